```python
import math
import functools
import jax
import jax.numpy as jnp
from jax import lax
import numpy as np

D_MODEL = 1024
BATCH = 1
SEQ = 16384
DEPTH = 4

GRID_W = 64
CTX_LEN = 256

N_BRANCH = 4
BRANCH_W = 512

S5_H = 16
S5_P = 64
S5_G = BRANCH_W // S5_H

GLA_HEADS = 4
GLA_DK = 64
GLA_DV = BRANCH_W // GLA_HEADS
GLA_QK = GLA_HEADS * GLA_DK
GLA_RANK = 16
GLA_TAU = 16.0
GLA_CHUNK = 64

LRU_BLOCKS = 8
LRU_BW = BRANCH_W // LRU_BLOCKS
LRU_CONV = 4
LRU_CONV_LEFT = 2
LRU_C = 8.0

RW_N = 64
RW_HEADS = BRANCH_W // RW_N
RW_DECAY_LORA = 64
RW_AAA_LORA = 64
RW_GATE_LORA = 128
RW_GN_EPS = 64e-5
RW_DECAY_SCALE = math.exp(-0.5)

FFN_HIDDEN = 2816
FFN_CONV = 3

DN_ALPHA = (2 * DEPTH) ** 0.25
DN_BETA = (8 * DEPTH) ** -0.25
LN_EPS = 1e-6

PROJ_SIZES = (BRANCH_W, GLA_QK, GLA_QK, BRANCH_W, BRANCH_W, BRANCH_W, BRANCH_W, BRANCH_W, BRANCH_W, BRANCH_W)
W_IN = 2 * GLA_QK + 8 * BRANCH_W

kernel_name = 'hybrid_s5_gla_rglru_rwkv7_dit_block'


def layer_norm(x, g, b, eps=LN_EPS):
    xf = x.astype(jnp.float32)
    mu = jnp.mean(xf, -1, keepdims=True)
    var = jnp.mean(jnp.square(xf - mu), -1, keepdims=True)
    y = ((xf - mu) * lax.rsqrt(var + eps)).astype(x.dtype)
    return y if g is None else y * g + b


def rms_norm(x, w, eps=LN_EPS):
    xf = x.astype(jnp.float32)
    y = (xf * lax.rsqrt(jnp.mean(jnp.square(xf), -1, keepdims=True) + eps)).astype(x.dtype)
    return y * w


def _rows(x, grid):
    if not grid:
        return x
    b, n, ch = x.shape
    rows = n // GRID_W
    return x.reshape(b * rows, GRID_W, ch)


def dwconv(x, w, b, left, grid):
    xs = _rows(x, grid)
    n = xs.shape[1]
    width = w.shape[0]
    xp = jnp.pad(xs, ((0, 0), (left, width - 1 - left), (0, 0)))
    y = b
    for k in range(width):
        y = y + w[k] * xp[:, k:k + n]
    return y.reshape(x.shape)


def token_shift(x, grid):
    xs = _rows(x, grid)
    xp = jnp.pad(xs, ((0, 0), (1, 1), (0, 0)))
    return (0.5 * (xp[:, :-2] + xp[:, 2:])).reshape(x.shape)


def _flip(t):
    return jnp.flip(t, axis=1)


def _identity(t):
    return t


def ctx_then_latent(scan_fn, ctx_args, lat_args, s0, reverse):
    f = _flip if reverse else _identity
    y_ctx, s_ctx = scan_fn(*[f(t) for t in ctx_args], s0)
    y_lat, _ = scan_fn(*[f(t) for t in lat_args], s_ctx)
    return jax.tree_util.tree_map(f, y_ctx), jax.tree_util.tree_map(f, y_lat)


def split_cols(t):
    out, start = [], 0
    for size in PROJ_SIZES:
        out.append(t[..., start:start + size])
        start += size
    return out


def cmul(a, b):
    return (a[0] * b[0] - a[1] * b[1], a[0] * b[1] + a[1] * b[0])


def s5_zoh(lam_re, lam_im, log_step):
    dt = jnp.exp(log_step)[:, None]
    mag = jnp.exp(lam_re * dt)
    a_bar = (mag * jnp.cos(lam_im * dt), mag * jnp.sin(lam_im * dt))
    num_re, num_im = a_bar[0] - 1.0, a_bar[1]
    den = lam_re * lam_re + lam_im * lam_im
    factor = ((num_re * lam_re + num_im * lam_im) / den, (num_im * lam_re - num_re * lam_im) / den)
    return a_bar, factor


def s5_scan(a_bar, bu_re, bu_im, s0):
    first = cmul(a_bar, s0)
    bu_re = bu_re.at[:, 0].add(first[0])
    bu_im = bu_im.at[:, 0].add(first[1])
    a_re = jnp.broadcast_to(a_bar[0], bu_re.shape)
    a_im = jnp.broadcast_to(a_bar[1], bu_re.shape)

    def combine(e1, e2):
        a = cmul(e2[:2], e1[:2])
        ab = cmul(e2[:2], e1[2:])
        return (a[0], a[1], ab[0] + e2[2], ab[1] + e2[3])

    _, _, h_re, h_im = lax.associative_scan(combine, (a_re, a_im, bu_re, bu_im), axis=1)
    return (h_re, h_im), (h_re[:, -1], h_im[:, -1])


def s5_mixer(u_c, u_l, lam_re, lam_im, log_step, b_re, b_im, c_re, c_im, d_skip, w_glu):
    def drive(u):
        ug = u.reshape(u.shape[0], u.shape[1], S5_G, S5_H)
        return (jnp.einsum('blgh,gph->blgp', ug, b_re), jnp.einsum('blgh,gph->blgp', ug, b_im))

    bu_c, bu_l = drive(u_c), drive(u_l)
    zero = jnp.zeros((u_c.shape[0], S5_G, S5_P), u_c.dtype)
    s0 = (zero, zero)
    outs = []
    for dr in range(2):
        a_bar, factor = s5_zoh(lam_re[dr], lam_im[dr], log_step[dr])
        outs.append(ctx_then_latent(functools.partial(s5_scan, a_bar), cmul(factor, bu_c),
                                    cmul(factor, bu_l), s0, reverse=dr == 1))

    def readout(h_f, h_b, u):
        h_re, h_im = h_f[0] + h_b[0], h_f[1] + h_b[1]
        y = jnp.einsum('blgp,ghp->blgh', h_re, c_re) - jnp.einsum('blgp,ghp->blgh', h_im, c_im)
        y = jax.nn.gelu(y.reshape(u.shape) + d_skip * u)
        return y * jax.nn.sigmoid(y @ w_glu)

    return (readout(outs[0][0], outs[1][0], u_c), readout(outs[0][1], outs[1][1], u_l))


def gla_chunked(q, k, v, log_a, s0):
    bsz, n_tok, nh, dk = q.shape
    dv = v.shape[-1]
    n_chunk = n_tok // GLA_CHUNK
    q, k, log_a = [t.reshape(bsz, n_chunk, GLA_CHUNK, nh, dk) for t in (q, k, log_a)]
    v = v.reshape(bsz, n_chunk, GLA_CHUNK, nh, dv)
    b = jnp.cumsum(log_a, axis=2)
    b_last = b[:, :, -1]
    b_mid = b[:, :, GLA_CHUNK // 2][:, :, None]
    att = jnp.einsum('bnihk,bnjhk->bnhij', q * jnp.exp(b - b_mid), k * jnp.exp(b_mid - b))
    mask = jnp.tril(jnp.ones((GLA_CHUNK, GLA_CHUNK), dtype=bool))
    att = jnp.where(mask, att, 0.0)
    o_intra = jnp.einsum('bnhij,bnjhv->bnihv', att, v)
    ds = jnp.einsum('bnjhk,bnjhv->bnhkv', k * jnp.exp(b_last[:, :, None] - b), v)
    decay = jnp.exp(b_last)

    def step(s, inp):
        dcy, d = inp
        return dcy[..., None] * s + d, s

    s_final, s_prev = lax.scan(step, s0, (jnp.moveaxis(decay, 1, 0), jnp.moveaxis(ds, 1, 0)))
    s_prev = jnp.moveaxis(s_prev, 0, 1)
    o_inter = jnp.einsum('bnihk,bnhkv->bnihv', q * jnp.exp(b), s_prev)
    return (o_intra + o_inter).reshape(bsz, n_tok, nh, dv), s_final


def gla_mixer(q_c, k_c, v_c, g_c, q_l, k_l, v_l, g_l, w1, w2, bias, norm_w):
    def heads(t, d):
        return t.reshape(t.shape[0], t.shape[1], GLA_HEADS, d)

    def prep(q, k, v):
        return (heads(q * GLA_DK ** -0.5, GLA_DK), heads(k, GLA_DK), heads(v, GLA_DV))

    def log_decay(k, dr):
        return heads(jax.nn.log_sigmoid((k @ w1[dr]) @ w2[dr] + bias[dr]) / GLA_TAU, GLA_DK)

    ctx, lat = prep(q_c, k_c, v_c), prep(q_l, k_l, v_l)
    s0 = jnp.zeros((q_c.shape[0], GLA_HEADS, GLA_DK, GLA_DV), q_c.dtype)
    outs = [ctx_then_latent(gla_chunked, ctx + (log_decay(k_c, dr),), lat + (log_decay(k_l, dr),),
                            s0, reverse=dr == 1) for dr in range(2)]

    def readout(o, g):
        return rms_norm(o, norm_w).reshape(g.shape) * jax.nn.silu(g)

    return (readout(outs[0][0] + outs[1][0], g_c), readout(outs[0][1] + outs[1][1], g_l))


def lru_scan(a, b, s0):
    b = b.at[:, 0].add(a[:, 0] * s0)
    _, h = lax.associative_scan(lambda e1, e2: (e2[0] * e1[0], e2[0] * e1[1] + e2[1]), (a, b), axis=1)
    return h, h[:, -1]


def lru_mixer(x_c, gate_c, x_l, gate_l, conv_w, conv_b, w_a, b_a, w_x, b_x, lam):
    xc = dwconv(x_c, conv_w, conv_b, LRU_CONV_LEFT, grid=False)
    xl = dwconv(x_l, conv_w, conv_b, LRU_CONV_LEFT, grid=True)

    def block_diag(x, w):
        xb = x.reshape(x.shape[0], x.shape[1], LRU_BLOCKS, LRU_BW)
        return jnp.einsum('blni,nij->blnj', xb, w).reshape(x.shape)

    def gates(x, dr):
        r = jax.nn.sigmoid(block_diag(x, w_a[dr]) + b_a[dr])
        i = jax.nn.sigmoid(block_diag(x, w_x[dr]) + b_x[dr])
        log_a = -LRU_C * jax.nn.softplus(-lam[dr]) * r
        return (jnp.exp(log_a), jnp.sqrt(-jnp.expm1(2.0 * log_a)) * (i * x))

    s0 = jnp.zeros((x_c.shape[0], BRANCH_W), x_c.dtype)
    outs = [ctx_then_latent(lru_scan, gates(xc, dr), gates(xl, dr), s0, reverse=dr == 1) for dr in range(2)]
    return ((outs[0][0] + outs[1][0]) * jax.nn.gelu(gate_c), (outs[0][1] + outs[1][1]) * jax.nn.gelu(gate_l))


def rwkv7_scan(r, w, k, v, a, b, s0):
    def step(s, inp):
        rt, wt, kt, vt, at, bt = inp
        sa = jnp.einsum('bhij,bhj->bhi', s, at)
        s = s * wt[:, :, None, :] + sa[..., None] * bt[:, :, None, :] + vt[..., None] * kt[:, :, None, :]
        return s, jnp.einsum('bhij,bhj->bhi', s, rt)

    s_last, y = lax.scan(step, s0, tuple(jnp.moveaxis(t, 1, 0) for t in (r, w, k, v, a, b)))
    return jnp.moveaxis(y, 0, 1), s_last


def rwkv7_mixer(r_c, k_c, v_c, r_l, k_l, v_l, mu, w0, w1, w2, a0, a1, a2, g1, g2, k_k, k_a, r_k, ln_w, ln_b):
    def heads(t):
        return t.reshape(t.shape[0], t.shape[1], RW_HEADS, RW_N)

    def prep(r, k, v, grid):
        r, k, v = [t + mu[j] * (token_shift(t, grid) - t) for j, t in enumerate((r, k, v))]
        kk = heads(k * k_k).astype(jnp.float32)
        kk = kk / jnp.maximum(jnp.sqrt(jnp.sum(kk * kk, -1, keepdims=True)), 1e-12)
        return r, k, v, kk.astype(k.dtype)

    def direction(r, k, v, kk, dr):
        w = jnp.exp(-RW_DECAY_SCALE * jax.nn.sigmoid(w0[dr] + jnp.tanh(k @ w1[dr]) @ w2[dr]))
        a = jax.nn.sigmoid(a0[dr] + (k @ a1[dr]) @ a2[dr])
        k_dir = k * (1.0 + (a - 1.0) * k_a)
        return (heads(r), heads(w), heads(k_dir), heads(v), -kk, kk * heads(a))

    ctx, lat = prep(r_c, k_c, v_c, False), prep(r_l, k_l, v_l, True)
    s0 = jnp.zeros((r_c.shape[0], RW_HEADS, RW_N, RW_N), r_c.dtype)
    outs = [ctx_then_latent(rwkv7_scan, direction(*ctx, dr), direction(*lat, dr), s0, reverse=dr == 1)
            for dr in range(2)]

    def readout(y, r, k, v):
        y = layer_norm(y, ln_w.reshape(RW_HEADS, RW_N), ln_b.reshape(RW_HEADS, RW_N), eps=RW_GN_EPS)
        bonus = jnp.sum(heads(r * k * r_k), -1, keepdims=True) * heads(v)
        g = jax.nn.sigmoid(r @ g1) @ g2
        return (y + bonus).reshape(r.shape) * g

    return (readout(outs[0][0] + outs[1][0], *ctx[:3]), readout(outs[0][1] + outs[1][1], *lat[:3]))


def token_mixing(h_c, h_l, p):
    pc, pl = split_cols(h_c @ p['w_in']), split_cols(h_l @ p['w_in'])
    ya = s5_mixer(pc[0], pl[0], p['s5_lam_re'], p['s5_lam_im'], p['s5_log_step'], p['s5_b_re'],
                  p['s5_b_im'], p['s5_c_re'], p['s5_c_im'], p['s5_d'], p['s5_w_glu'])
    yb = gla_mixer(pc[1], pc[2], pc[3], pc[4], pl[1], pl[2], pl[3], pl[4],
                   p['gla_w1'], p['gla_w2'], p['gla_bias'], p['gla_norm_w'])
    yc = lru_mixer(pc[5], pc[6], pl[5], pl[6], p['lru_conv_w'], p['lru_conv_b'], p['lru_w_a'],
                   p['lru_b_a'], p['lru_w_x'], p['lru_b_x'], p['lru_lam'])
    yd = rwkv7_mixer(pc[7], pc[8], pc[9], pl[7], pl[8], pl[9], p['rw_mu'], p['rw_w0'], p['rw_w1'],
                     p['rw_w2'], p['rw_a0'], p['rw_a1'], p['rw_a2'], p['rw_g1'], p['rw_g2'],
                     p['rw_k_k'], p['rw_k_a'], p['rw_r_k'], p['rw_ln_w'], p['rw_ln_b'])
    return (ya[0], yb[0], yc[0], yd[0]), (ya[1], yb[1], yc[1], yd[1])


def merge(h, ys, p):
    gates = jax.nn.sigmoid(h @ p['w_gate'] + p['b_gate'])
    z = 0.0
    for n in range(N_BRANCH):
        z = z + gates[..., n * D_MODEL:(n + 1) * D_MODEL] * (ys[n] @ p['w_branch'][n])
    return z @ p['w_out']


def conv_ffn(h, p, grid):
    u, v = jnp.split(h @ p['ffn_w_up'], 2, axis=-1)
    u = dwconv(u, p['ffn_conv_w'], p['ffn_conv_b'], 1, grid)
    return (jax.nn.gelu(u) * v) @ p['ffn_w_down']


def trunk_layer(x_c, x_l, c, c_ctx, p, last):
    mod_l = (jax.nn.silu(c) @ p['w_mod'] + p['b_mod'])[:, None, :]
    mod_c = jax.nn.silu(c_ctx) @ p['w_mod'] + p['b_mod']
    sh1_l, sc1_l, g1_l, sh2_l, sc2_l, g2_l = jnp.split(mod_l, 6, axis=-1)
    sh1_c, sc1_c, g1_c, sh2_c, sc2_c, g2_c = jnp.split(mod_c, 6, axis=-1)
    h_l = x_l * (1.0 + sc1_l) + sh1_l
    h_c = x_c * (1.0 + sc1_c) + sh1_c
    ys_c, ys_l = token_mixing(h_c, h_l, p)
    x_l = layer_norm(DN_ALPHA * x_l + g1_l * merge(h_l, ys_l, p), p['ln1_g'], p['ln1_b'])
    f_l = conv_ffn(x_l * (1.0 + sc2_l) + sh2_l, p, grid=True)
    x_l = layer_norm(DN_ALPHA * x_l + g2_l * f_l, p['ln2_g'], p['ln2_b'])
    if not last:
        x_c = layer_norm(DN_ALPHA * x_c + g1_c * merge(h_c, ys_c, p), p['ln1_g'], p['ln1_b'])
        f_c = conv_ffn(x_c * (1.0 + sc2_c) + sh2_c, p, grid=False)
        x_c = layer_norm(DN_ALPHA * x_c + g2_c * f_c, p['ln2_g'], p['ln2_b'])
    return x_c, x_l


def setup_inputs(seed: int = 0) -> dict:
    key = jax.random.key(seed)
    keys = iter(jax.random.split(key, 96))
    f32 = jnp.float32

    def nrm(shape, scale):
        return scale * jax.random.normal(next(keys), shape, f32)

    def unif(shape, lo, hi):
        return jax.random.uniform(next(keys), shape, f32, lo, hi)

    def gain(shape):
        return 1.0 + nrm(shape, 0.02)

    L, D, W, F = DEPTH, D_MODEL, BRANCH_W, FFN_HIDDEN
    lru_p = unif((L, 2, W), 0.9, 0.999) ** (1.0 / LRU_C)
    return {
        'x': nrm((BATCH, SEQ, D), 1.0),
        'c': nrm((BATCH, D), 1.0),
        'ctx': nrm((BATCH, CTX_LEN, D), 1.0),
        'c_ctx': nrm((D,), 1.0),
        'w_mod': nrm((L, D, 6 * D), 0.5 * D ** -0.5),
        'b_mod': nrm((L, 6 * D), 0.02),
        'w_in': nrm((L, D, W_IN), D ** -0.5),
        's5_lam_re': -0.5 + nrm((L, 2, S5_G, S5_P), 0.01),
        's5_lam_im': math.pi * jnp.arange(S5_P, dtype=f32) + nrm((L, 2, S5_G, S5_P), 0.01),
        's5_log_step': unif((L, 2, S5_G), math.log(1e-3), math.log(1e-1)),
        's5_b_re': nrm((L, S5_G, S5_P, S5_H), (2 * S5_H) ** -0.5),
        's5_b_im': nrm((L, S5_G, S5_P, S5_H), (2 * S5_H) ** -0.5),
        's5_c_re': nrm((L, S5_G, S5_H, S5_P), S5_P ** -0.5),
        's5_c_im': nrm((L, S5_G, S5_H, S5_P), S5_P ** -0.5),
        's5_d': nrm((L, W), 1.0),
        's5_w_glu': nrm((L, W, W), W ** -0.5),
        'gla_w1': nrm((L, 2, GLA_QK, GLA_RANK), GLA_QK ** -0.5),
        'gla_w2': nrm((L, 2, GLA_RANK, GLA_QK), GLA_RANK ** -0.5),
        'gla_bias': 2.0 + nrm((L, 2, GLA_QK), 0.5),
        'gla_norm_w': gain((L, GLA_DV)),
        'lru_conv_w': nrm((L, LRU_CONV, W), LRU_CONV ** -0.5),
        'lru_conv_b': nrm((L, W), 0.02),
        'lru_w_a': nrm((L, 2, LRU_BLOCKS, LRU_BW, LRU_BW), LRU_BW ** -0.5),
        'lru_b_a': nrm((L, 2, W), 0.02),
        'lru_w_x': nrm((L, 2, LRU_BLOCKS, LRU_BW, LRU_BW), LRU_BW ** -0.5),
        'lru_b_x': nrm((L, 2, W), 0.02),
        'lru_lam': jnp.log(lru_p) - jnp.log1p(-lru_p),
        'rw_mu': unif((L, 3, W), 0.0, 1.0),
        'rw_w0': nrm((L, 2, W), 0.5),
        'rw_w1': nrm((L, 2, W, RW_DECAY_LORA), W ** -0.5),
        'rw_w2': nrm((L, 2, RW_DECAY_LORA, W), RW_DECAY_LORA ** -0.5),
        'rw_a0': nrm((L, 2, W), 0.5),
        'rw_a1': nrm((L, 2, W, RW_AAA_LORA), W ** -0.5),
        'rw_a2': nrm((L, 2, RW_AAA_LORA, W), RW_AAA_LORA ** -0.5),
        'rw_g1': nrm((L, W, RW_GATE_LORA), W ** -0.5),
        'rw_g2': nrm((L, RW_GATE_LORA, W), RW_GATE_LORA ** -0.5),
        'rw_k_k': 0.85 + nrm((L, W), 0.02),
        'rw_k_a': gain((L, W)),
        'rw_r_k': nrm((L, W), 0.1),
        'rw_ln_w': gain((L, W)),
        'rw_ln_b': nrm((L, W), 0.02),
        'w_branch': nrm((L, N_BRANCH, W, D), W ** -0.5),
        'w_gate': nrm((L, D, N_BRANCH * D), D ** -0.5),
        'b_gate': nrm((L, N_BRANCH * D), 0.02),
        'w_out': nrm((L, D, D), DN_BETA * D ** -0.5),
        'ln1_g': gain((L, D)),
        'ln1_b': nrm((L, D), 0.02),
        'ln2_g': gain((L, D)),
        'ln2_b': nrm((L, D), 0.02),
        'ffn_w_up': nrm((L, D, 2 * F), D ** -0.5),
        'ffn_conv_w': nrm((L, FFN_CONV, F), FFN_CONV ** -0.5),
        'ffn_conv_b': nrm((L, F), 0.02),
        'ffn_w_down': nrm((L, F, D), DN_BETA * F ** -0.5),
    }


def reference(x, c, ctx, c_ctx, w_mod, b_mod, w_in,
              s5_lam_re, s5_lam_im, s5_log_step, s5_b_re, s5_b_im, s5_c_re, s5_c_im, s5_d, s5_w_glu,
              gla_w1, gla_w2, gla_bias, gla_norm_w,
              lru_conv_w, lru_conv_b, lru_w_a, lru_b_a, lru_w_x, lru_b_x, lru_lam,
              rw_mu, rw_w0, rw_w1, rw_w2, rw_a0, rw_a1, rw_a2, rw_g1, rw_g2, rw_k_k, rw_k_a, rw_r_k,
              rw_ln_w, rw_ln_b,
              w_branch, w_gate, b_gate, w_out, ln1_g, ln1_b, ln2_g, ln2_b,
              ffn_w_up, ffn_conv_w, ffn_conv_b, ffn_w_down):
    layer_params = dict(
        w_mod=w_mod, b_mod=b_mod, w_in=w_in,
        s5_lam_re=s5_lam_re, s5_lam_im=s5_lam_im, s5_log_step=s5_log_step, s5_b_re=s5_b_re,
        s5_b_im=s5_b_im, s5_c_re=s5_c_re, s5_c_im=s5_c_im, s5_d=s5_d, s5_w_glu=s5_w_glu,
        gla_w1=gla_w1, gla_w2=gla_w2, gla_bias=gla_bias, gla_norm_w=gla_norm_w,
        lru_conv_w=lru_conv_w, lru_conv_b=lru_conv_b, lru_w_a=lru_w_a, lru_b_a=lru_b_a,
        lru_w_x=lru_w_x, lru_b_x=lru_b_x, lru_lam=lru_lam,
        rw_mu=rw_mu, rw_w0=rw_w0, rw_w1=rw_w1, rw_w2=rw_w2, rw_a0=rw_a0, rw_a1=rw_a1, rw_a2=rw_a2,
        rw_g1=rw_g1, rw_g2=rw_g2, rw_k_k=rw_k_k, rw_k_a=rw_k_a, rw_r_k=rw_r_k,
        rw_ln_w=rw_ln_w, rw_ln_b=rw_ln_b,
        w_branch=w_branch, w_gate=w_gate, b_gate=b_gate, w_out=w_out,
        ln1_g=ln1_g, ln1_b=ln1_b, ln2_g=ln2_g, ln2_b=ln2_b,
        ffn_w_up=ffn_w_up, ffn_conv_w=ffn_conv_w, ffn_conv_b=ffn_conv_b, ffn_w_down=ffn_w_down)
    x_l = layer_norm(x, None, None)
    x_c = layer_norm(ctx, None, None)
    for i in range(DEPTH):
        p = {name: t[i] for name, t in layer_params.items()}
        x_c, x_l = trunk_layer(x_c, x_l, c, c_ctx, p, last=(i == DEPTH - 1))
    return x_l
```

```python
import functools
import math

import numpy as np
import jax
import jax.numpy as jnp
from jax import lax
from jax.experimental import pallas as pl
from jax.experimental.pallas import tpu as pltpu

F32 = jnp.float32
BF16 = jnp.bfloat16

D_MODEL = 1024
GRID_W = 64
CTX_LEN = 256
TILE = CTX_LEN
BRANCH_W = 512
N_BRANCH = 4

S5_H, S5_P = 16, 64
S5_G = BRANCH_W // S5_H
S5_STATE = S5_G * S5_P
S5_SUB = 8
S5_STEPS = TILE // S5_SUB

GLA_HEADS, GLA_DK = 4, 64
GLA_DV = BRANCH_W // GLA_HEADS
GLA_QK = GLA_HEADS * GLA_DK
GLA_TAU = 16.0
CHUNK = 64
N_CHUNK = TILE // CHUNK

LRU_C = 8.0
LRU_CONV_LEFT = 2

RW_N = 64
RW_GROUP = 256
RW_GN_EPS = 64e-5
RW_DECAY_SCALE = math.exp(-0.5)

FFN_HIDDEN = 2816
FFN_CHUNK = 1408
LORA_PAD = 128

LN_EPS = 1e-6
V7X_VMEM_LIMIT_BYTES = 56 * 1024 * 1024


def _bf(x):
    return x.astype(BF16)


def _mm(a, b):
    return jnp.dot(_bf(a), _bf(b), preferred_element_type=F32)


def _mm_nt(a, b):
    return lax.dot_general(_bf(a), _bf(b), (((1,), (1,)), ((), ())), preferred_element_type=F32)


def _mm_tn(a, b):
    return lax.dot_general(_bf(a), _bf(b), (((0,), (0,)), ((), ())), preferred_element_type=F32)


def _split2(x):
    hi = _bf(x)
    lo = _bf(x - hi.astype(F32))
    return hi, lo


def _split3(x):
    hi = _bf(x)
    r = x - hi.astype(F32)
    mid = _bf(r)
    lo = _bf(r - mid.astype(F32))
    return hi, mid, lo


def _mm_hp(a, b):
    ah, al = _split2(a)
    bh, bl = _split2(b)
    dot = functools.partial(jnp.dot, preferred_element_type=F32)
    return dot(ah, bh) + (dot(ah, bl) + dot(al, bh))


def _mm_exact_rhs(a, b_exact):
    ah, al = _split2(a)
    dot = functools.partial(jnp.dot, preferred_element_type=F32)
    return dot(ah, b_exact) + dot(al, b_exact)


def _iota(shape, dim):
    return lax.broadcasted_iota(jnp.int32, shape, dim)


def _sigmoid(x):
    return 1.0 / (1.0 + jnp.exp(-x))


def _silu(x):
    return x * _sigmoid(x)


def _gelu(x):
    return 0.5 * x * (1.0 + jnp.tanh(math.sqrt(2.0 / math.pi) * (x + 0.044715 * (x * x * x))))


def _log_sigmoid(x):
    return jnp.minimum(x, 0.0) - jnp.log1p(jnp.exp(-jnp.abs(x)))


def _expm1(x):
    u = jnp.exp(x)
    um1 = u - 1.0
    small = jnp.where(u == 1.0, x, um1 * x / jnp.log(jnp.where(u == 1.0, 2.0, u)))
    return jnp.where(jnp.abs(x) > 0.5, um1, small)


def _layer_norm(x, eps):
    mu = jnp.mean(x, axis=-1, keepdims=True)
    xc = x - mu
    var = jnp.mean(xc * xc, axis=-1, keepdims=True)
    return xc * lax.rsqrt(var + eps)


def _shift_rows(x, k, is_ctx):
    n = x.shape[0]
    t = _iota(x.shape, 0)
    pos = jnp.where(is_ctx, t, t & (GRID_W - 1))
    row_len = jnp.where(is_ctx, n, GRID_W)
    rolled = pltpu.roll(x, (-k) % n, axis=0)
    valid = (pos + k >= 0) & (pos + k < row_len)
    return jnp.where(valid, rolled, 0.0)


def _seg_cumsum(x, reverse):
    n = x.shape[0]
    t = _iota(x.shape, 0) & (CHUNK - 1)
    d = 1
    while d < CHUNK:
        if reverse:
            x = x + jnp.where(t < CHUNK - d, pltpu.roll(x, n - d, axis=0), 0.0)
        else:
            x = x + jnp.where(t >= d, pltpu.roll(x, d, axis=0), 0.0)
        d *= 2
    return x


def _scan_tile(a, b, reverse):
    n = a.shape[0]
    t = _iota(a.shape, 0)
    d = 1
    while d < n:
        if reverse:
            valid, sh = t < n - d, n - d
        else:
            valid, sh = t >= d, d
        a_s = jnp.where(valid, pltpu.roll(a, sh, axis=0), 1.0)
        b_s = jnp.where(valid, pltpu.roll(b, sh, axis=0), 0.0)
        b = b + a * b_s
        a = a * a_s
        d *= 2
    return a, b


def _chunk_rows(x, row):
    parts = [jnp.broadcast_to(x[c * CHUNK + row:c * CHUNK + row + 1], (CHUNK, x.shape[1]))
             for c in range(x.shape[0] // CHUNK)]
    return jnp.concatenate(parts, axis=0)


def _tile4(x):
    return jnp.concatenate([x] * (RW_GROUP // CHUNK), axis=0)


def _block_mask(rows, cols, row_blk, col_blk):
    return (_iota((rows, cols), 0) // row_blk) == (_iota((rows, cols), 1) // col_blk)


def _causal_cat(reverse, strict):
    t = _iota((CHUNK, RW_GROUP), 0)
    s = _iota((CHUNK, RW_GROUP), 1) & (CHUNK - 1)
    if reverse:
        return (s > t) if strict else (s >= t)
    return (s < t) if strict else (s <= t)


def _bwd_tile(i, n_tiles):
    return jnp.where(i == 0, 0, n_tiles - i)


def _cparams():
    return pltpu.CompilerParams(dimension_semantics=("arbitrary",), vmem_limit_bytes=V7X_VMEM_LIMIT_BYTES)


def _full(arr):
    nd = arr.ndim
    return pl.BlockSpec(arr.shape, lambda i, _nd=nd: (0,) * _nd)


def _ln0_kernel(x_ref, o_ref):
    o_ref[...] = _layer_norm(x_ref[...], LN_EPS)


def _ln0(xcat):
    n_tok = xcat.shape[0]
    return pl.pallas_call(
        _ln0_kernel,
        grid=(n_tok // TILE,),
        in_specs=[pl.BlockSpec((TILE, D_MODEL), lambda i: (i, 0))],
        out_specs=pl.BlockSpec((TILE, D_MODEL), lambda i: (i, 0)),
        out_shape=jax.ShapeDtypeStruct((n_tok, D_MODEL), F32),
        compiler_params=_cparams(),
        name="ln0",
    )(xcat)


MOD_COLS = 1536


def _mod_kernel(c_ref, w_ref, b_ref, o_ref):
    s = _silu(c_ref[...])
    o_ref[0] = _mm_hp(s, w_ref[0]) + b_ref[0]


def _modulation(cc, w_mod, b_mod):
    depth = w_mod.shape[0]
    n_col = w_mod.shape[2] // MOD_COLS
    return pl.pallas_call(
        _mod_kernel,
        grid=(depth, n_col),
        in_specs=[pl.BlockSpec((8, D_MODEL), lambda l, j: (0, 0)),
                  pl.BlockSpec((1, D_MODEL, MOD_COLS), lambda l, j: (l, 0, j)),
                  pl.BlockSpec((1, 1, MOD_COLS), lambda l, j: (l, 0, j))],
        out_specs=pl.BlockSpec((1, 8, MOD_COLS), lambda l, j: (l, 0, j)),
        out_shape=jax.ShapeDtypeStruct((depth, 8, w_mod.shape[2]), F32),
        compiler_params=pltpu.CompilerParams(dimension_semantics=("arbitrary", "arbitrary"),
                                             vmem_limit_bytes=V7X_VMEM_LIMIT_BYTES),
        name="modulation",
    )(cc, w_mod, b_mod.reshape(depth, 1, -1))


def _mod_rows(mod_ref, is_ctx):
    m = mod_ref[...]
    return jnp.where(is_ctx, m[1:2], m[0:1])


def _mod_part(m, j):
    return m[:, j * D_MODEL:(j + 1) * D_MODEL]


def _inproj_kernel(x_ref, mod_ref, w_ref, o_ref):
    m = _mod_rows(mod_ref, pl.program_id(0) == 0)
    h = x_ref[...] * (1.0 + _mod_part(m, 1)) + _mod_part(m, 0)
    o_ref[...] = _mm(h, w_ref[...])


def _inproj(xs, mod, w_in):
    n_tok = xs.shape[0]
    w_cols = w_in.shape[1]
    return pl.pallas_call(
        _inproj_kernel,
        grid=(n_tok // TILE,),
        in_specs=[pl.BlockSpec((TILE, D_MODEL), lambda i: (i, 0)), _full(mod), _full(w_in)],
        out_specs=pl.BlockSpec((TILE, w_cols), lambda i: (i, 0)),
        out_shape=jax.ShapeDtypeStruct((n_tok, w_cols), F32),
        compiler_params=_cparams(),
        name="inproj",
    )(xs, mod, w_in)


COL_S5_U, COL_GLA_QK, COL_GLA_V, COL_GLA_G, COL_LRU_X, COL_LRU_GATE, COL_RW_R, COL_RW_K, COL_RW_V = (
    0, 1, 2, 3, 4, 5, 6, 7, 8)


def _proj_specs(n_tiles, col):
    fwd = pl.BlockSpec((TILE, BRANCH_W), lambda i: (i, col))
    bwd = pl.BlockSpec((TILE, BRANCH_W), lambda i: (_bwd_tile(i, n_tiles), col))
    return fwd, bwd


def _out_specs(n_tiles):
    fwd = pl.BlockSpec((TILE, BRANCH_W), lambda i: (i, 0))
    bwd = pl.BlockSpec((TILE, BRANCH_W), lambda i: (_bwd_tile(i, n_tiles), 0))
    return fwd, bwd


def _lru_kernel(xf_ref, xb_ref, cw_ref, cb_ref, wa_ref, ba_ref, wx_ref, bx_ref, lc_ref,
                hf_ref, hb_ref, carry_ref):
    is_ctx = pl.program_id(0) == 0

    @pl.when(is_ctx)
    def _():
        carry_ref[...] = jnp.zeros_like(carry_ref)

    cw = cw_ref[...]
    for dr, (x_ref, o_ref) in enumerate(((xf_ref, hf_ref), (xb_ref, hb_ref))):
        x = x_ref[...]
        xc = cb_ref[...] + sum(cw[k:k + 1] * _shift_rows(x, k - LRU_CONV_LEFT, is_ctx) for k in range(cw.shape[0]))
        r = _sigmoid(_mm(xc, wa_ref[dr]) + ba_ref[dr:dr + 1])
        ig = _sigmoid(_mm(xc, wx_ref[dr]) + bx_ref[dr:dr + 1])
        log_a = lc_ref[dr:dr + 1] * r
        a = jnp.exp(log_a)
        b = jnp.sqrt(-_expm1(2.0 * log_a)) * (ig * xc)
        a_cum, h0 = _scan_tile(a, b, reverse=dr == 1)
        h = h0 + a_cum * carry_ref[dr:dr + 1]
        o_ref[...] = h
        last = 0 if dr == 1 else TILE - 1
        carry_ref[dr:dr + 1] = h[last:last + 1]


def _lru(proj, n_tiles, p):
    xf, xb = _proj_specs(n_tiles, COL_LRU_X)
    of, ob = _out_specs(n_tiles)
    n_tok = proj.shape[0]
    consts = (p['lru_conv_w'], p['lru_conv_b'], p['lru_wa'], p['lru_b_a'], p['lru_wx'], p['lru_b_x'], p['lru_c'])
    out = jax.ShapeDtypeStruct((n_tok, BRANCH_W), F32)
    return pl.pallas_call(
        _lru_kernel,
        grid=(n_tiles,),
        in_specs=[xf, xb] + [_full(c) for c in consts],
        out_specs=[of, ob],
        out_shape=[out, out],
        scratch_shapes=[pltpu.VMEM((8, BRANCH_W), F32)],
        compiler_params=_cparams(),
        name="lru_scan",
    )(proj, proj, *consts)


S5_HALF = S5_STATE // 2


def _cmul(ar, ai, br, bi):
    return ar * br - ai * bi, ar * bi + ai * br


def _s5_kernel(uf_ref, ub_ref, perm_ref, permt_ref, a_ref, drive_ref, read_ref,
               yf_ref, yb_ref, h_ref, carry_ref):
    @pl.when(pl.program_id(0) == 0)
    def _():
        carry_ref[...] = jnp.zeros_like(carry_ref)

    sub = _iota((S5_SUB, S5_HALF), 0)
    for dr, (u_ref, o_ref) in enumerate(((uf_ref, yf_ref), (ub_ref, yb_ref))):
        u_p = jnp.dot(perm_ref[dr], _bf(u_ref[...]), preferred_element_type=F32)
        h_ref[dr] = _mm(u_p, drive_ref[dr])
        for half in range(2):
            re = slice(half * S5_HALF, (half + 1) * S5_HALF)
            im = slice(S5_STATE + half * S5_HALF, S5_STATE + (half + 1) * S5_HALF)
            a_re = a_ref[dr, :, re]
            a_im = a_ref[dr, :, im]
            a_re8 = jnp.broadcast_to(a_re, (S5_SUB, S5_HALF))
            a_im8 = jnp.broadcast_to(a_im, (S5_SUB, S5_HALF))

            def local_step(j, carry, dr=dr, re=re, im=im, a_re8=a_re8, a_im8=a_im8):
                h_re, h_im = carry
                rows = pl.ds(pl.multiple_of(j * S5_SUB, S5_SUB), S5_SUB)
                m_re, m_im = _cmul(a_re8, a_im8, h_re, h_im)
                h_re = m_re + h_ref[dr, rows, re]
                h_im = m_im + h_ref[dr, rows, im]
                h_ref[dr, rows, re] = h_re
                h_ref[dr, rows, im] = h_im
                return h_re, h_im

            zero = jnp.zeros((S5_SUB, S5_HALF), F32)
            end_re, end_im = lax.fori_loop(0, S5_STEPS, local_step, (zero, zero))

            p_re, p_im = a_re, a_im
            for _ in range(int(math.log2(S5_STEPS))):
                p_re, p_im = _cmul(p_re, p_im, p_re, p_im)
            s_re = carry_ref[dr, :, re]
            s_im = carry_ref[dr, :, im]
            in_re = jnp.zeros((S5_SUB, S5_HALF), F32)
            in_im = jnp.zeros((S5_SUB, S5_HALF), F32)
            for s in range(S5_SUB):
                in_re = jnp.where(sub == s, s_re, in_re)
                in_im = jnp.where(sub == s, s_im, in_im)
                m_re, m_im = _cmul(p_re, p_im, s_re, s_im)
                s_re = m_re + end_re[s:s + 1]
                s_im = m_im + end_im[s:s + 1]
            carry_ref[dr, :, re] = s_re
            carry_ref[dr, :, im] = s_im

            def fix_step(j, carry, dr=dr, re=re, im=im, a_re8=a_re8, a_im8=a_im8, in_re=in_re, in_im=in_im):
                q_re, q_im = carry
                rows = pl.ds(pl.multiple_of(j * S5_SUB, S5_SUB), S5_SUB)
                c_re, c_im = _cmul(q_re, q_im, in_re, in_im)
                h_ref[dr, rows, re] = h_ref[dr, rows, re] + c_re
                h_ref[dr, rows, im] = h_ref[dr, rows, im] + c_im
                return _cmul(q_re, q_im, a_re8, a_im8)

            lax.fori_loop(0, S5_STEPS, fix_step, (a_re8, a_im8))

        y_p = _mm(h_ref[dr], read_ref[...])
        hi, mid, lo = _split3(y_p)
        dot = functools.partial(jnp.dot, preferred_element_type=F32)
        o_ref[...] = (dot(permt_ref[dr], hi) + dot(permt_ref[dr], mid)) + dot(permt_ref[dr], lo)


def _s5(proj, n_tiles, p):
    uf, ub = _proj_specs(n_tiles, COL_S5_U)
    of, ob = _out_specs(n_tiles)
    n_tok = proj.shape[0]
    consts = (p['s5_perm'], p['s5_perm_t'], p['s5_a'], p['s5_drive'], p['s5_read'])
    out = jax.ShapeDtypeStruct((n_tok, BRANCH_W), F32)
    return pl.pallas_call(
        _s5_kernel,
        grid=(n_tiles,),
        in_specs=[uf, ub] + [_full(c) for c in consts],
        out_specs=[of, ob],
        out_shape=[out, out],
        scratch_shapes=[pltpu.VMEM((2, TILE, 2 * S5_STATE), F32), pltpu.VMEM((2, 1, 2 * S5_STATE), F32)],
        compiler_params=_cparams(),
        name="s5_scan",
    )(proj, proj, *consts)


def _gla_kernel(qkf_ref, vf_ref, qkb_ref, vb_ref, w1_ref, w2_ref, bias_ref, of_ref, ob_ref, st_ref):
    @pl.when(pl.program_id(0) == 0)
    def _():
        st_ref[...] = jnp.zeros_like(st_ref)

    bm_k = _block_mask(RW_GROUP, GLA_QK, CHUNK, GLA_DK)
    bm_v = _block_mask(RW_GROUP, BRANCH_W, CHUNK, GLA_DV)
    bm_s = _block_mask(BRANCH_W, GLA_QK, GLA_DV, GLA_DK)
    for dr, (qk_ref, v_ref, o_ref) in enumerate(((qkf_ref, vf_ref, of_ref), (qkb_ref, vb_ref, ob_ref))):
        reverse = dr == 1
        q = qk_ref[:, :GLA_QK] * (GLA_DK ** -0.5)
        k = qk_ref[:, GLA_QK:]
        v = v_ref[...]
        la = _log_sigmoid(_mm_hp(_mm_hp(k, w1_ref[dr]), w2_ref[dr]) + bias_ref[dr:dr + 1]) * (1.0 / GLA_TAU)
        b = _seg_cumsum(la, reverse)
        b_end = _chunk_rows(b, 0 if reverse else CHUNK - 1)
        b_mid = _chunk_rows(b, CHUNK - 1 - CHUNK // 2 if reverse else CHUNK // 2)
        qe = q * jnp.exp(b - b_mid)
        ke = k * jnp.exp(b_mid - b)
        kl = k * jnp.exp(b_end - b)
        qb = q * jnp.exp(b)
        decay = jnp.exp(b_end)
        causal = _causal_cat(reverse, strict=False)
        st = st_ref[dr]
        order = range(N_CHUNK - 1, -1, -1) if reverse else range(N_CHUNK)
        for c in order:
            rows = slice(c * CHUNK, (c + 1) * CHUNK)
            ke_blk = jnp.where(bm_k, _tile4(ke[rows]), 0.0)
            att = jnp.where(causal, _mm_nt(qe[rows], ke_blk), 0.0)
            v_blk = jnp.where(bm_v, _tile4(v[rows]), 0.0)
            o_ref[rows, :] = _mm(att, v_blk) + _mm_nt(qb[rows], st)
            st = st * decay[c * CHUNK:c * CHUNK + 1] + jnp.where(bm_s, _mm_tn(v[rows], kl[rows]), 0.0)
        st_ref[dr] = st


def _gla(proj, n_tiles, p):
    qkf, qkb = _proj_specs(n_tiles, COL_GLA_QK)
    vf, vb = _proj_specs(n_tiles, COL_GLA_V)
    of, ob = _out_specs(n_tiles)
    n_tok = proj.shape[0]
    consts = (p['gla_w1'], p['gla_w2'], p['gla_bias'])
    out = jax.ShapeDtypeStruct((n_tok, BRANCH_W), F32)
    return pl.pallas_call(
        _gla_kernel,
        grid=(n_tiles,),
        in_specs=[qkf, vf, qkb, vb] + [_full(c) for c in consts],
        out_specs=[of, ob],
        out_shape=[out, out],
        scratch_shapes=[pltpu.VMEM((2, BRANCH_W, GLA_QK), F32)],
        compiler_params=_cparams(),
        name="gla_scan",
    )(proj, proj, proj, proj, *consts)


def _seg_sum(x, ones_blk):
    return _mm_exact_rhs(x, ones_blk)


def _rw_prepare(r_ref, k_ref, v_ref, mu_ref, kk_ref, ones_ref, is_ctx):
    def mix(x, j):
        xs = 0.5 * (_shift_rows(x, -1, is_ctx) + _shift_rows(x, 1, is_ctx))
        return x + mu_ref[j:j + 1] * (xs - x)

    r, k, v = mix(r_ref[...], 0), mix(k_ref[...], 1), mix(v_ref[...], 2)
    kk = k * kk_ref[...]
    kk = kk / jnp.maximum(jnp.sqrt(_seg_sum(kk * kk, ones_ref[...])), 1e-12)
    return r, k, v, kk


def _rw_chunk(a_t, r_t, b_t, k_t, a_true, r_true, b_hat, k_hat, v, gamma_end, zt, reverse, bm, eye_cat):
    def blk(x):
        return jnp.where(bm, _tile4(x), 0.0)

    strict = _causal_cat(reverse, strict=True)
    incl = _causal_cat(reverse, strict=False)
    g = _mm_nt(jnp.concatenate([a_t, r_t], axis=0), jnp.concatenate([blk(b_t), blk(k_t)], axis=0))
    g_ab = jnp.where(strict, g[:CHUNK, :RW_GROUP], 0.0)
    g_ak = jnp.where(strict, g[:CHUNK, RW_GROUP:], 0.0)
    g_rb = jnp.where(incl, g[CHUNK:, :RW_GROUP], 0.0)
    g_rk = jnp.where(incl, g[CHUNK:, RW_GROUP:], 0.0)
    t_inv = eye_cat + g_ab
    l_pow = g_ab
    for _ in range(int(math.log2(CHUNK)) - 1):
        l_pow = _mm(l_pow, blk(l_pow))
        t_inv = t_inv + _mm(t_inv, blk(l_pow))
    v_blk = blk(v)
    gv = _mm(g_ak, v_blk)
    wu = _mm(t_inv, jnp.concatenate([blk(a_true), blk(gv)], axis=1))
    w_mat, u_loc = wu[:, :RW_GROUP], wu[:, RW_GROUP:]
    gw = _mm(g_rb, blk(w_mat))
    y_loc = _mm(jnp.concatenate([g_rb, g_rk], axis=1), jnp.concatenate([blk(u_loc), v_blk], axis=0))
    rw = _mm_nt(jnp.concatenate([r_true + gw, w_mat], axis=0), zt)
    y = rw[:CHUNK] + y_loc
    u = rw[CHUNK:] + u_loc
    upd = _mm_tn(jnp.concatenate([u, v], axis=0), jnp.concatenate([b_hat, k_hat], axis=0))
    zt = zt * gamma_end + jnp.where(bm, upd, 0.0)
    return y, zt


def _rwkv_kernel(rf_ref, kf_ref, vf_ref, rb_ref, kb_ref, vb_ref,
                 mu_ref, kkw_ref, ka_ref, rk_ref, ones_ref,
                 w0_ref, w1_ref, w2_ref, a0_ref, a1_ref, a2_ref, g1_ref, g2_ref,
                 yf_ref, yb_ref, bonus_ref, gate_ref, zt_ref):
    is_ctx = pl.program_id(0) == 0

    @pl.when(is_ctx)
    def _():
        zt_ref[...] = jnp.zeros_like(zt_ref)

    bm = _block_mask(RW_GROUP, RW_GROUP, CHUNK, RW_N)
    eye_cat = (_iota((CHUNK, RW_GROUP), 0) == (_iota((CHUNK, RW_GROUP), 1) & (CHUNK - 1))).astype(F32)
    n_group = BRANCH_W // RW_GROUP
    streams = ((rf_ref, kf_ref, vf_ref, yf_ref), (rb_ref, kb_ref, vb_ref, yb_ref))
    for dr, (r_ref, k_ref, v_ref, y_ref) in enumerate(streams):
        reverse = dr == 1
        r, k, v, kk = _rw_prepare(r_ref, k_ref, v_ref, mu_ref, kkw_ref, ones_ref, is_ctx)
        if dr == 0:
            bonus_ref[...] = _seg_sum(r * k * rk_ref[...], ones_ref[...]) * v
            gate_ref[...] = _mm(_sigmoid(_mm(r, g1_ref[...])), g2_ref[...])
        lw = -RW_DECAY_SCALE * _sigmoid(w0_ref[dr:dr + 1] + _mm_hp(jnp.tanh(_mm_hp(k, w1_ref[dr])), w2_ref[dr]))
        ag = _sigmoid(a0_ref[dr:dr + 1] + _mm_hp(_mm_hp(k, a1_ref[dr]), a2_ref[dr]))
        kd = k * (1.0 + (ag - 1.0) * ka_ref[...])
        a = -kk
        b = kk * ag
        g = _seg_cumsum(lw, reverse)
        g_prev = g - lw
        g_end = _chunk_rows(g, 0 if reverse else CHUNK - 1)
        g_mid = _chunk_rows(g, CHUNK - 1 - CHUNK // 2 if reverse else CHUNK // 2)
        a_t = a * jnp.exp(g_prev - g_mid)
        r_t = r * jnp.exp(g - g_mid)
        e_mid = jnp.exp(g_mid - g)
        b_t = b * e_mid
        k_t = kd * e_mid
        a_true = a * jnp.exp(g_prev)
        r_true = r * jnp.exp(g)
        e_end = jnp.exp(g_end - g)
        b_hat = b * e_end
        k_hat = kd * e_end
        gamma_end = jnp.exp(g_end)
        order = range(N_CHUNK - 1, -1, -1) if reverse else range(N_CHUNK)
        for grp in range(n_group):
            cols = slice(grp * RW_GROUP, (grp + 1) * RW_GROUP)
            zt = zt_ref[dr, grp]
            for c in order:
                rows = slice(c * CHUNK, (c + 1) * CHUNK)
                y, zt = _rw_chunk(a_t[rows, cols], r_t[rows, cols], b_t[rows, cols], k_t[rows, cols],
                                  a_true[rows, cols], r_true[rows, cols], b_hat[rows, cols], k_hat[rows, cols],
                                  v[rows, cols], gamma_end[c * CHUNK:c * CHUNK + 1, cols], zt, reverse, bm, eye_cat)
                y_ref[rows, cols] = y
            zt_ref[dr, grp] = zt


def _rwkv(proj, n_tiles, p):
    rf, rb = _proj_specs(n_tiles, COL_RW_R)
    kf, kb = _proj_specs(n_tiles, COL_RW_K)
    vf, vb = _proj_specs(n_tiles, COL_RW_V)
    of, ob = _out_specs(n_tiles)
    n_tok = proj.shape[0]
    consts = (p['rw_mu'], p['rw_k_k'], p['rw_k_a'], p['rw_r_k'], p['ones64'],
              p['rw_w0'], p['rw_w1'], p['rw_w2'], p['rw_a0'], p['rw_a1'], p['rw_a2'], p['rw_g1'], p['rw_g2'])
    out = jax.ShapeDtypeStruct((n_tok, BRANCH_W), F32)
    return pl.pallas_call(
        _rwkv_kernel,
        grid=(n_tiles,),
        in_specs=[rf, kf, vf, rb, kb, vb] + [_full(c) for c in consts],
        out_specs=[of, ob, of, of],
        out_shape=[out, out, out, out],
        scratch_shapes=[pltpu.VMEM((2, BRANCH_W // RW_GROUP, RW_GROUP, RW_GROUP), F32)],
        compiler_params=_cparams(),
        name="rwkv_scan",
    )(proj, proj, proj, proj, proj, proj, *consts)


def _merge_kernel(x_ref, mod_ref, u_ref, gg_ref, lg_ref,
                  s5f_ref, s5b_ref, glf_ref, glb_ref, lrf_ref, lrb_ref, rwf_ref, rwb_ref, bonus_ref, rgate_ref,
                  s5d_ref, glu_ref, gnw_ref, ones64_ref, ones128_ref, lnw_ref, lnb_ref,
                  wg_ref, bg_ref, wb_ref, wo_ref, ln1g_ref, ln1b_ref, o_ref, *, dn_alpha):
    m = _mod_rows(mod_ref, pl.program_id(0) == 0)
    x = x_ref[...]
    h = _bf(x * (1.0 + _mod_part(m, 1)) + _mod_part(m, 0))

    ya = _gelu(s5f_ref[...] + s5b_ref[...] + s5d_ref[...] * u_ref[...])
    ya = ya * _sigmoid(_mm(ya, glu_ref[...]))

    o = glf_ref[...] + glb_ref[...]
    ms = _seg_sum(o * o, ones128_ref[...]) * (1.0 / GLA_DV)
    yb = (o * lax.rsqrt(ms + LN_EPS)) * gnw_ref[...] * _silu(gg_ref[...])

    yc = (lrf_ref[...] + lrb_ref[...]) * _gelu(lg_ref[...])

    y = rwf_ref[...] + rwb_ref[...]
    mu = _seg_sum(y, ones64_ref[...]) * (1.0 / RW_N)
    yc0 = y - mu
    var = _seg_sum(yc0 * yc0, ones64_ref[...]) * (1.0 / RW_N)
    yn = (yc0 * lax.rsqrt(var + RW_GN_EPS)) * lnw_ref[...] + lnb_ref[...]
    yd = (yn + bonus_ref[...]) * rgate_ref[...]

    z = jnp.zeros((TILE, D_MODEL), F32)
    for n, yn_ in enumerate((ya, yb, yc, yd)):
        cols = slice(n * D_MODEL, (n + 1) * D_MODEL)
        gate = _sigmoid(jnp.dot(h, wg_ref[:, cols], preferred_element_type=F32) + bg_ref[:, cols])
        z = z + gate * _mm(yn_, wb_ref[n])
    out = _mm(z, wo_ref[...])
    x1 = dn_alpha * x + _mod_part(m, 2) * out
    o_ref[...] = _layer_norm(x1, LN_EPS) * ln1g_ref[...] + ln1b_ref[...]


def _merge(xs, mod, proj, parts, n_tiles, p, dn_alpha):
    n_tok = xs.shape[0]
    tile_d = pl.BlockSpec((TILE, D_MODEL), lambda i: (i, 0))
    tile_w = pl.BlockSpec((TILE, BRANCH_W), lambda i: (i, 0))

    def pcol(col):
        return pl.BlockSpec((TILE, BRANCH_W), lambda i: (i, col))

    consts = (p['s5_d'], p['s5_w_glu'], p['gla_norm_w'], p['ones64'], p['ones128'], p['rw_ln_w'], p['rw_ln_b'],
              p['w_gate'], p['b_gate'], p['w_branch'], p['w_out'], p['ln1_g'], p['ln1_b'])
    return pl.pallas_call(
        functools.partial(_merge_kernel, dn_alpha=dn_alpha),
        grid=(n_tiles,),
        in_specs=[tile_d, _full(mod), pcol(COL_S5_U), pcol(COL_GLA_G), pcol(COL_LRU_GATE)]
                 + [tile_w] * len(parts) + [_full(c) for c in consts],
        out_specs=tile_d,
        out_shape=jax.ShapeDtypeStruct((n_tok, D_MODEL), F32),
        compiler_params=_cparams(),
        name="merge_ln1",
    )(xs, mod, proj, proj, proj, *parts, *consts)


def _ffn_kernel(x_ref, mod_ref, wup_ref, cw_ref, cb_ref, wdn_ref, ln2g_ref, ln2b_ref, o_ref, *, dn_alpha):
    is_ctx = pl.program_id(0) == 0
    m = _mod_rows(mod_ref, is_ctx)
    x = x_ref[...]
    h = _bf(x * (1.0 + _mod_part(m, 4)) + _mod_part(m, 3))
    acc = jnp.zeros((TILE, D_MODEL), F32)
    for j in range(FFN_HIDDEN // FFN_CHUNK):
        ucols = slice(j * FFN_CHUNK, (j + 1) * FFN_CHUNK)
        vcols = slice(FFN_HIDDEN + j * FFN_CHUNK, FFN_HIDDEN + (j + 1) * FFN_CHUNK)
        u = jnp.dot(h, wup_ref[:, ucols], preferred_element_type=F32)
        v = jnp.dot(h, wup_ref[:, vcols], preferred_element_type=F32)
        uc = cb_ref[:, ucols] + sum(cw_ref[k:k + 1, ucols] * _shift_rows(u, k - 1, is_ctx) for k in range(3))
        acc = acc + _mm(_gelu(uc) * v, wdn_ref[ucols, :])
    x2 = dn_alpha * x + _mod_part(m, 5) * acc
    o_ref[...] = _layer_norm(x2, LN_EPS) * ln2g_ref[...] + ln2b_ref[...]


def _ffn(xs, mod, p, dn_alpha):
    n_tok = xs.shape[0]
    tile_d = pl.BlockSpec((TILE, D_MODEL), lambda i: (i, 0))
    consts = (p['ffn_w_up'], p['ffn_conv_w'], p['ffn_conv_b'], p['ffn_w_down'], p['ln2_g'], p['ln2_b'])
    return pl.pallas_call(
        functools.partial(_ffn_kernel, dn_alpha=dn_alpha),
        grid=(n_tok // TILE,),
        in_specs=[tile_d, _full(mod)] + [_full(c) for c in consts],
        out_specs=tile_d,
        out_shape=jax.ShapeDtypeStruct((n_tok, D_MODEL), F32),
        compiler_params=_cparams(),
        name="convffn_ln2",
    )(xs, mod, *consts)


def _block_diag(blocks):
    n, a, b = blocks.shape
    eye = jnp.eye(n, dtype=blocks.dtype)
    return (eye[:, None, :, None] * blocks[:, :, None, :]).reshape(n * a, n * b)


def _pad_cols(w, n):
    return jnp.pad(w, [(0, 0)] * (w.ndim - 1) + [(0, n - w.shape[-1])])


def _pad_rows(w, n):
    return jnp.pad(w, [(0, 0)] * (w.ndim - 2) + [(0, n - w.shape[-2]), (0, 0)])


def _s5_perms():
    perm = np.zeros((2, TILE, TILE), np.float32)
    for j in range(S5_STEPS):
        for s in range(S5_SUB):
            t = s * S5_STEPS + j
            perm[0, j * S5_SUB + s, t] = 1.0
            perm[1, j * S5_SUB + s, TILE - 1 - t] = 1.0
    return jnp.asarray(perm, BF16), jnp.asarray(np.transpose(perm, (0, 2, 1)), BF16)


def _ones_blocks(width):
    idx = np.arange(BRANCH_W) // width
    return jnp.asarray(idx[:, None] == idx[None, :], BF16)


def _layer_params(q, i):
    p = {}
    row = lambda v: v[i].reshape(1, -1)
    dt = jnp.exp(q['s5_log_step'][i])[:, :, None]
    lam_re, lam_im = q['s5_lam_re'][i], q['s5_lam_im'][i]
    mag = jnp.exp(lam_re * dt)
    a_re, a_im = mag * jnp.cos(lam_im * dt), mag * jnp.sin(lam_im * dt)
    den = lam_re * lam_re + lam_im * lam_im
    f_re = ((a_re - 1.0) * lam_re + a_im * lam_im) / den
    f_im = (a_im * lam_re - (a_re - 1.0) * lam_im) / den
    b_re, b_im = q['s5_b_re'][i], q['s5_b_im'][i]
    d_re = f_re[..., None] * b_re[None] - f_im[..., None] * b_im[None]
    d_im = f_re[..., None] * b_im[None] + f_im[..., None] * b_re[None]
    to_drive = lambda d: jnp.stack([_block_diag(jnp.swapaxes(d[dr], 1, 2)) for dr in range(2)])
    p['s5_drive'] = jnp.concatenate([to_drive(d_re), to_drive(d_im)], axis=-1).astype(BF16)
    p['s5_a'] = jnp.concatenate([a_re.reshape(2, 1, -1), a_im.reshape(2, 1, -1)], axis=-1)
    c_re, c_im = q['s5_c_re'][i], q['s5_c_im'][i]
    p['s5_read'] = jnp.concatenate([_block_diag(jnp.swapaxes(c_re, 1, 2)),
                                    -_block_diag(jnp.swapaxes(c_im, 1, 2))], axis=0).astype(BF16)
    p['s5_perm'], p['s5_perm_t'] = _s5_perms()
    p['s5_d'] = row(q['s5_d'])
    p['s5_w_glu'] = q['s5_w_glu'][i].astype(BF16)
    p['gla_w1'] = _pad_cols(q['gla_w1'][i], LORA_PAD)
    p['gla_w2'] = _pad_rows(q['gla_w2'][i], LORA_PAD)
    p['gla_bias'] = q['gla_bias'][i]
    p['gla_norm_w'] = jnp.tile(q['gla_norm_w'][i], GLA_HEADS).reshape(1, -1)
    p['lru_conv_w'] = q['lru_conv_w'][i]
    p['lru_conv_b'] = row(q['lru_conv_b'])
    p['lru_wa'] = jnp.stack([_block_diag(q['lru_w_a'][i, dr]) for dr in range(2)]).astype(BF16)
    p['lru_wx'] = jnp.stack([_block_diag(q['lru_w_x'][i, dr]) for dr in range(2)]).astype(BF16)
    p['lru_b_a'], p['lru_b_x'] = q['lru_b_a'][i], q['lru_b_x'][i]
    p['lru_c'] = -LRU_C * jax.nn.softplus(-q['lru_lam'][i])
    p['rw_mu'] = q['rw_mu'][i]
    for name in ('rw_k_k', 'rw_k_a', 'rw_r_k', 'rw_ln_w', 'rw_ln_b'):
        p[name] = row(q[name])
    p['rw_w0'], p['rw_a0'] = q['rw_w0'][i], q['rw_a0'][i]
    p['rw_w1'] = _pad_cols(q['rw_w1'][i], LORA_PAD)
    p['rw_w2'] = _pad_rows(q['rw_w2'][i], LORA_PAD)
    p['rw_a1'] = _pad_cols(q['rw_a1'][i], LORA_PAD)
    p['rw_a2'] = _pad_rows(q['rw_a2'][i], LORA_PAD)
    p['rw_g1'] = q['rw_g1'][i].astype(BF16)
    p['rw_g2'] = q['rw_g2'][i].astype(BF16)
    p['ones64'] = _ones_blocks(RW_N)
    p['ones128'] = _ones_blocks(GLA_DV)
    p['w_in'] = q['w_in'][i].astype(BF16)
    p['w_gate'] = q['w_gate'][i].astype(BF16)
    p['b_gate'] = row(q['b_gate'])
    p['w_branch'] = q['w_branch'][i].astype(BF16)
    p['w_out'] = q['w_out'][i].astype(BF16)
    p['ffn_w_up'] = q['ffn_w_up'][i].astype(BF16)
    p['ffn_w_down'] = q['ffn_w_down'][i].astype(BF16)
    p['ffn_conv_w'] = q['ffn_conv_w'][i]
    p['ffn_conv_b'] = row(q['ffn_conv_b'])
    for name in ('ln1_g', 'ln1_b', 'ln2_g', 'ln2_b'):
        p[name] = row(q[name])
    return p


def kernel(x, c, ctx, c_ctx, w_mod, b_mod, w_in, s5_lam_re, s5_lam_im, s5_log_step, s5_b_re, s5_b_im, s5_c_re, s5_c_im, s5_d, s5_w_glu, gla_w1, gla_w2, gla_bias, gla_norm_w, lru_conv_w, lru_conv_b, lru_w_a, lru_b_a, lru_w_x, lru_b_x, lru_lam, rw_mu, rw_w0, rw_w1, rw_w2, rw_a0, rw_a1, rw_a2, rw_g1, rw_g2, rw_k_k, rw_k_a, rw_r_k, rw_ln_w, rw_ln_b, w_branch, w_gate, b_gate, w_out, ln1_g, ln1_b, ln2_g, ln2_b, ffn_w_up, ffn_conv_w, ffn_conv_b, ffn_w_down):
    q = dict(w_in=w_in, s5_lam_re=s5_lam_re, s5_lam_im=s5_lam_im, s5_log_step=s5_log_step, s5_b_re=s5_b_re,
             s5_b_im=s5_b_im, s5_c_re=s5_c_re, s5_c_im=s5_c_im, s5_d=s5_d, s5_w_glu=s5_w_glu,
             gla_w1=gla_w1, gla_w2=gla_w2, gla_bias=gla_bias, gla_norm_w=gla_norm_w,
             lru_conv_w=lru_conv_w, lru_conv_b=lru_conv_b, lru_w_a=lru_w_a, lru_b_a=lru_b_a,
             lru_w_x=lru_w_x, lru_b_x=lru_b_x, lru_lam=lru_lam,
             rw_mu=rw_mu, rw_w0=rw_w0, rw_w1=rw_w1, rw_w2=rw_w2, rw_a0=rw_a0, rw_a1=rw_a1, rw_a2=rw_a2,
             rw_g1=rw_g1, rw_g2=rw_g2, rw_k_k=rw_k_k, rw_k_a=rw_k_a, rw_r_k=rw_r_k,
             rw_ln_w=rw_ln_w, rw_ln_b=rw_ln_b,
             w_branch=w_branch, w_gate=w_gate, b_gate=b_gate, w_out=w_out,
             ln1_g=ln1_g, ln1_b=ln1_b, ln2_g=ln2_g, ln2_b=ln2_b,
             ffn_w_up=ffn_w_up, ffn_conv_w=ffn_conv_w, ffn_conv_b=ffn_conv_b, ffn_w_down=ffn_w_down)
    batch, seq, d = x.shape
    assert batch == 1 and d == D_MODEL and ctx.shape[1] == CTX_LEN and seq % TILE == 0
    depth = w_mod.shape[0]
    dn_alpha = (2 * depth) ** 0.25
    n_tiles = 1 + seq // TILE

    cc = jnp.zeros((8, D_MODEL), F32).at[0].set(c[0]).at[1].set(c_ctx)
    mods = _modulation(cc, w_mod, b_mod)
    xs = _ln0(jnp.concatenate([ctx[0], x[0]], axis=0))
    for i in range(depth):
        p = _layer_params(q, i)
        mod = mods[i]
        proj = _inproj(xs, mod, p['w_in'])
        s5f, s5b = _s5(proj, n_tiles, p)
        glf, glb = _gla(proj, n_tiles, p)
        lrf, lrb = _lru(proj, n_tiles, p)
        rwf, rwb, bonus, rgate = _rwkv(proj, n_tiles, p)
        xs = _merge(xs, mod, proj, (s5f, s5b, glf, glb, lrf, lrb, rwf, rwb, bonus, rgate), n_tiles, p, dn_alpha)
        xs = _ffn(xs, mod, p, dn_alpha)
    return xs[TILE:].reshape(batch, seq, d)
```

```python
import functools
import math

import numpy as np
import jax
import jax.numpy as jnp
from jax import lax
from jax.experimental import pallas as pl
from jax.experimental.pallas import tpu as pltpu

F32 = jnp.float32
BF16 = jnp.bfloat16

D_MODEL = 1024
GRID_W = 64
CTX_LEN = 256
TILE = CTX_LEN
BRANCH_W = 512
N_BRANCH = 4

S5_H, S5_P = 16, 64
S5_G = BRANCH_W // S5_H
S5_STATE = S5_G * S5_P
S5_SUB = 8
S5_STEPS = TILE // S5_SUB

GLA_HEADS, GLA_DK = 4, 64
GLA_DV = BRANCH_W // GLA_HEADS
GLA_QK = GLA_HEADS * GLA_DK
GLA_TAU = 16.0
CHUNK = 64
N_CHUNK = TILE // CHUNK

LRU_C = 8.0
LRU_CONV_LEFT = 2

RW_N = 64
RW_GROUP = 256
RW_GN_EPS = 64e-5
RW_DECAY_SCALE = math.exp(-0.5)

FFN_HIDDEN = 2816
FFN_CHUNK = 1408
LORA_PAD = 128

LN_EPS = 1e-6
V7X_VMEM_LIMIT_BYTES = 56 * 1024 * 1024


def _bf(x):
    return x.astype(BF16)


def _mm(a, b):
    return jnp.dot(_bf(a), _bf(b), preferred_element_type=F32)


def _mm_nt(a, b):
    return lax.dot_general(_bf(a), _bf(b), (((1,), (1,)), ((), ())), preferred_element_type=F32)


def _mm_tn(a, b):
    return lax.dot_general(_bf(a), _bf(b), (((0,), (0,)), ((), ())), preferred_element_type=F32)


def _split2(x):
    hi = _bf(x)
    lo = _bf(x - hi.astype(F32))
    return hi, lo


def _split3(x):
    hi = _bf(x)
    r = x - hi.astype(F32)
    mid = _bf(r)
    lo = _bf(r - mid.astype(F32))
    return hi, mid, lo


def _mm_hp(a, b):
    ah, al = _split2(a)
    bh, bl = _split2(b)
    dot = functools.partial(jnp.dot, preferred_element_type=F32)
    return dot(ah, bh) + (dot(ah, bl) + dot(al, bh))


def _mm_exact_rhs(a, b_exact):
    ah, al = _split2(a)
    dot = functools.partial(jnp.dot, preferred_element_type=F32)
    return dot(ah, b_exact) + dot(al, b_exact)


def _iota(shape, dim):
    return lax.broadcasted_iota(jnp.int32, shape, dim)


def _sigmoid(x):
    return 1.0 / (1.0 + jnp.exp(-x))


def _silu(x):
    return x * _sigmoid(x)


def _gelu(x):
    return 0.5 * x * (1.0 + jnp.tanh(math.sqrt(2.0 / math.pi) * (x + 0.044715 * (x * x * x))))


def _log_sigmoid(x):
    return jnp.minimum(x, 0.0) - jnp.log1p(jnp.exp(-jnp.abs(x)))


def _expm1(x):
    u = jnp.exp(x)
    um1 = u - 1.0
    small = jnp.where(u == 1.0, x, um1 * x / jnp.log(jnp.where(u == 1.0, 2.0, u)))
    return jnp.where(jnp.abs(x) > 0.5, um1, small)


def _layer_norm(x, eps):
    mu = jnp.mean(x, axis=-1, keepdims=True)
    xc = x - mu
    var = jnp.mean(xc * xc, axis=-1, keepdims=True)
    return xc * lax.rsqrt(var + eps)


def _shift_rows(x, k, is_ctx):
    n = x.shape[0]
    t = _iota(x.shape, 0)
    pos = jnp.where(is_ctx, t, t & (GRID_W - 1))
    row_len = jnp.where(is_ctx, n, GRID_W)
    rolled = pltpu.roll(x, (-k) % n, axis=0)
    valid = (pos + k >= 0) & (pos + k < row_len)
    return jnp.where(valid, rolled, 0.0)


def _seg_cumsum(x, reverse):
    n = x.shape[0]
    t = _iota(x.shape, 0) & (CHUNK - 1)
    d = 1
    while d < CHUNK:
        if reverse:
            x = x + jnp.where(t < CHUNK - d, pltpu.roll(x, n - d, axis=0), 0.0)
        else:
            x = x + jnp.where(t >= d, pltpu.roll(x, d, axis=0), 0.0)
        d *= 2
    return x


def _scan_tile(a, b, reverse):
    n = a.shape[0]
    t = _iota(a.shape, 0)
    d = 1
    while d < n:
        if reverse:
            valid, sh = t < n - d, n - d
        else:
            valid, sh = t >= d, d
        a_s = jnp.where(valid, pltpu.roll(a, sh, axis=0), 1.0)
        b_s = jnp.where(valid, pltpu.roll(b, sh, axis=0), 0.0)
        b = b + a * b_s
        a = a * a_s
        d *= 2
    return a, b


def _chunk_rows(x, row):
    parts = [jnp.broadcast_to(x[c * CHUNK + row:c * CHUNK + row + 1], (CHUNK, x.shape[1]))
             for c in range(x.shape[0] // CHUNK)]
    return jnp.concatenate(parts, axis=0)


def _tile4(x):
    return jnp.concatenate([x] * (RW_GROUP // CHUNK), axis=0)


def _block_mask(rows, cols, row_blk, col_blk):
    return (_iota((rows, cols), 0) // row_blk) == (_iota((rows, cols), 1) // col_blk)


def _causal_cat(reverse, strict):
    t = _iota((CHUNK, RW_GROUP), 0)
    s = _iota((CHUNK, RW_GROUP), 1) & (CHUNK - 1)
    if reverse:
        return (s > t) if strict else (s >= t)
    return (s < t) if strict else (s <= t)


def _bwd_tile(i, n_tiles):
    return jnp.where(i == 0, 0, n_tiles - i)


def _cparams():
    return pltpu.CompilerParams(dimension_semantics=("arbitrary",), vmem_limit_bytes=V7X_VMEM_LIMIT_BYTES)


def _full(arr):
    nd = arr.ndim
    return pl.BlockSpec(arr.shape, lambda i, _nd=nd: (0,) * _nd)


def _ln0_kernel(x_ref, o_ref):
    o_ref[...] = _layer_norm(x_ref[...], LN_EPS)


def _ln0(xcat):
    n_tok = xcat.shape[0]
    return pl.pallas_call(
        _ln0_kernel,
        grid=(n_tok // TILE,),
        in_specs=[pl.BlockSpec((TILE, D_MODEL), lambda i: (i, 0))],
        out_specs=pl.BlockSpec((TILE, D_MODEL), lambda i: (i, 0)),
        out_shape=jax.ShapeDtypeStruct((n_tok, D_MODEL), F32),
        compiler_params=_cparams(),
        name="ln0",
    )(xcat)


MOD_COLS = 1536


def _mod_kernel(c_ref, w_ref, b_ref, o_ref):
    s = _silu(c_ref[...])
    o_ref[0] = _mm_hp(s, w_ref[0]) + b_ref[0]


def _modulation(cc, w_mod, b_mod):
    depth = w_mod.shape[0]
    n_col = w_mod.shape[2] // MOD_COLS
    return pl.pallas_call(
        _mod_kernel,
        grid=(depth, n_col),
        in_specs=[pl.BlockSpec((8, D_MODEL), lambda l, j: (0, 0)),
                  pl.BlockSpec((1, D_MODEL, MOD_COLS), lambda l, j: (l, 0, j)),
                  pl.BlockSpec((1, 1, MOD_COLS), lambda l, j: (l, 0, j))],
        out_specs=pl.BlockSpec((1, 8, MOD_COLS), lambda l, j: (l, 0, j)),
        out_shape=jax.ShapeDtypeStruct((depth, 8, w_mod.shape[2]), F32),
        compiler_params=pltpu.CompilerParams(dimension_semantics=("arbitrary", "arbitrary"),
                                             vmem_limit_bytes=V7X_VMEM_LIMIT_BYTES),
        name="modulation",
    )(cc, w_mod, b_mod.reshape(depth, 1, -1))


def _mod_rows(mod_ref, is_ctx):
    m = mod_ref[...]
    return jnp.where(is_ctx, m[1:2], m[0:1])


def _mod_part(m, j):
    return m[:, j * D_MODEL:(j + 1) * D_MODEL]


def _inproj_kernel(x_ref, mod_ref, w_ref, o_ref):
    m = _mod_rows(mod_ref, pl.program_id(0) == 0)
    h = x_ref[...] * (1.0 + _mod_part(m, 1)) + _mod_part(m, 0)
    o_ref[...] = _mm(h, w_ref[...])


def _inproj(xs, mod, w_in):
    n_tok = xs.shape[0]
    w_cols = w_in.shape[1]
    return pl.pallas_call(
        _inproj_kernel,
        grid=(n_tok // TILE,),
        in_specs=[pl.BlockSpec((TILE, D_MODEL), lambda i: (i, 0)), _full(mod), _full(w_in)],
        out_specs=pl.BlockSpec((TILE, w_cols), lambda i: (i, 0)),
        out_shape=jax.ShapeDtypeStruct((n_tok, w_cols), F32),
        compiler_params=_cparams(),
        name="inproj",
    )(xs, mod, w_in)


COL_S5_U, COL_GLA_QK, COL_GLA_V, COL_GLA_G, COL_LRU_X, COL_LRU_GATE, COL_RW_R, COL_RW_K, COL_RW_V = (
    0, 1, 2, 3, 4, 5, 6, 7, 8)


def _proj_specs(n_tiles, col):
    fwd = pl.BlockSpec((TILE, BRANCH_W), lambda i: (i, col))
    bwd = pl.BlockSpec((TILE, BRANCH_W), lambda i: (_bwd_tile(i, n_tiles), col))
    return fwd, bwd


def _out_specs(n_tiles):
    fwd = pl.BlockSpec((TILE, BRANCH_W), lambda i: (i, 0))
    bwd = pl.BlockSpec((TILE, BRANCH_W), lambda i: (_bwd_tile(i, n_tiles), 0))
    return fwd, bwd


def _lru_kernel(xf_ref, xb_ref, cw_ref, cb_ref, wa_ref, ba_ref, wx_ref, bx_ref, lc_ref,
                hf_ref, hb_ref, carry_ref):
    is_ctx = pl.program_id(0) == 0

    @pl.when(is_ctx)
    def _():
        carry_ref[...] = jnp.zeros_like(carry_ref)

    cw = cw_ref[...]
    for dr, (x_ref, o_ref) in enumerate(((xf_ref, hf_ref), (xb_ref, hb_ref))):
        x = x_ref[...]
        xc = cb_ref[...] + sum(cw[k:k + 1] * _shift_rows(x, k - LRU_CONV_LEFT, is_ctx) for k in range(cw.shape[0]))
        r = _sigmoid(_mm(xc, wa_ref[dr]) + ba_ref[dr:dr + 1])
        ig = _sigmoid(_mm(xc, wx_ref[dr]) + bx_ref[dr:dr + 1])
        log_a = lc_ref[dr:dr + 1] * r
        a = jnp.exp(log_a)
        b = jnp.sqrt(-_expm1(2.0 * log_a)) * (ig * xc)
        a_cum, h0 = _scan_tile(a, b, reverse=dr == 1)
        h = h0 + a_cum * carry_ref[dr:dr + 1]
        o_ref[...] = h
        last = 0 if dr == 1 else TILE - 1
        carry_ref[dr:dr + 1] = h[last:last + 1]


def _lru(proj, n_tiles, p):
    xf, xb = _proj_specs(n_tiles, COL_LRU_X)
    of, ob = _out_specs(n_tiles)
    n_tok = proj.shape[0]
    consts = (p['lru_conv_w'], p['lru_conv_b'], p['lru_wa'], p['lru_b_a'], p['lru_wx'], p['lru_b_x'], p['lru_c'])
    out = jax.ShapeDtypeStruct((n_tok, BRANCH_W), F32)
    return pl.pallas_call(
        _lru_kernel,
        grid=(n_tiles,),
        in_specs=[xf, xb] + [_full(c) for c in consts],
        out_specs=[of, ob],
        out_shape=[out, out],
        scratch_shapes=[pltpu.VMEM((8, BRANCH_W), F32)],
        compiler_params=_cparams(),
        name="lru_scan",
    )(proj, proj, *consts)


S5_HALF = S5_STATE // 2


def _cmul(ar, ai, br, bi):
    return ar * br - ai * bi, ar * bi + ai * br


def _s5_kernel(uf_ref, ub_ref, perm_ref, permt_ref, a_ref, drive_ref, read_ref,
               yf_ref, yb_ref, h_ref, carry_ref):
    @pl.when(pl.program_id(0) == 0)
    def _():
        carry_ref[...] = jnp.zeros_like(carry_ref)

    sub = _iota((S5_SUB, S5_HALF), 0)
    for dr, (u_ref, o_ref) in enumerate(((uf_ref, yf_ref), (ub_ref, yb_ref))):
        u_p = jnp.dot(perm_ref[dr], _bf(u_ref[...]), preferred_element_type=F32)
        h_ref[dr] = _mm(u_p, drive_ref[dr])
        for half in range(2):
            re = slice(half * S5_HALF, (half + 1) * S5_HALF)
            im = slice(S5_STATE + half * S5_HALF, S5_STATE + (half + 1) * S5_HALF)
            a_re = a_ref[dr, :, re]
            a_im = a_ref[dr, :, im]
            a_re8 = jnp.broadcast_to(a_re, (S5_SUB, S5_HALF))
            a_im8 = jnp.broadcast_to(a_im, (S5_SUB, S5_HALF))

            def local_step(j, carry, dr=dr, re=re, im=im, a_re8=a_re8, a_im8=a_im8):
                h_re, h_im = carry
                rows = pl.ds(pl.multiple_of(j * S5_SUB, S5_SUB), S5_SUB)
                m_re, m_im = _cmul(a_re8, a_im8, h_re, h_im)
                h_re = m_re + h_ref[dr, rows, re]
                h_im = m_im + h_ref[dr, rows, im]
                h_ref[dr, rows, re] = h_re
                h_ref[dr, rows, im] = h_im
                return h_re, h_im

            zero = jnp.zeros((S5_SUB, S5_HALF), F32)
            end_re, end_im = lax.fori_loop(0, S5_STEPS, local_step, (zero, zero))

            p_re, p_im = a_re, a_im
            for _ in range(int(math.log2(S5_STEPS))):
                p_re, p_im = _cmul(p_re, p_im, p_re, p_im)
            s_re = carry_ref[dr, :, re]
            s_im = carry_ref[dr, :, im]
            in_re = jnp.zeros((S5_SUB, S5_HALF), F32)
            in_im = jnp.zeros((S5_SUB, S5_HALF), F32)
            for s in range(S5_SUB):
                in_re = jnp.where(sub == s, s_re, in_re)
                in_im = jnp.where(sub == s, s_im, in_im)
                m_re, m_im = _cmul(p_re, p_im, s_re, s_im)
                s_re = m_re + end_re[s:s + 1]
                s_im = m_im + end_im[s:s + 1]
            carry_ref[dr, :, re] = s_re
            carry_ref[dr, :, im] = s_im

            def fix_step(j, carry, dr=dr, re=re, im=im, a_re8=a_re8, a_im8=a_im8, in_re=in_re, in_im=in_im):
                q_re, q_im = carry
                rows = pl.ds(pl.multiple_of(j * S5_SUB, S5_SUB), S5_SUB)
                c_re, c_im = _cmul(q_re, q_im, in_re, in_im)
                h_ref[dr, rows, re] = h_ref[dr, rows, re] + c_re
                h_ref[dr, rows, im] = h_ref[dr, rows, im] + c_im
                return _cmul(q_re, q_im, a_re8, a_im8)

            lax.fori_loop(0, S5_STEPS, fix_step, (a_re8, a_im8))

        y_p = _mm(h_ref[dr], read_ref[...])
        hi, mid, lo = _split3(y_p)
        dot = functools.partial(jnp.dot, preferred_element_type=F32)
        o_ref[...] = (dot(permt_ref[dr], hi) + dot(permt_ref[dr], mid)) + dot(permt_ref[dr], lo)


def _s5(proj, n_tiles, p):
    uf, ub = _proj_specs(n_tiles, COL_S5_U)
    of, ob = _out_specs(n_tiles)
    n_tok = proj.shape[0]
    consts = (p['s5_perm'], p['s5_perm_t'], p['s5_a'], p['s5_drive'], p['s5_read'])
    out = jax.ShapeDtypeStruct((n_tok, BRANCH_W), F32)
    return pl.pallas_call(
        _s5_kernel,
        grid=(n_tiles,),
        in_specs=[uf, ub] + [_full(c) for c in consts],
        out_specs=[of, ob],
        out_shape=[out, out],
        scratch_shapes=[pltpu.VMEM((2, TILE, 2 * S5_STATE), F32), pltpu.VMEM((2, 1, 2 * S5_STATE), F32)],
        compiler_params=_cparams(),
        name="s5_scan",
    )(proj, proj, *consts)


def _gla_kernel(qkf_ref, vf_ref, qkb_ref, vb_ref, w1_ref, w2_ref, bias_ref, of_ref, ob_ref, st_ref):
    @pl.when(pl.program_id(0) == 0)
    def _():
        st_ref[...] = jnp.zeros_like(st_ref)

    bm_k = _block_mask(RW_GROUP, GLA_QK, CHUNK, GLA_DK)
    bm_v = _block_mask(RW_GROUP, BRANCH_W, CHUNK, GLA_DV)
    bm_s = _block_mask(BRANCH_W, GLA_QK, GLA_DV, GLA_DK)
    for dr, (qk_ref, v_ref, o_ref) in enumerate(((qkf_ref, vf_ref, of_ref), (qkb_ref, vb_ref, ob_ref))):
        reverse = dr == 1
        q = qk_ref[:, :GLA_QK] * (GLA_DK ** -0.5)
        k = qk_ref[:, GLA_QK:]
        v = v_ref[...]
        la = _log_sigmoid(_mm_hp(_mm_hp(k, w1_ref[dr]), w2_ref[dr]) + bias_ref[dr:dr + 1]) * (1.0 / GLA_TAU)
        b = _seg_cumsum(la, reverse)
        b_end = _chunk_rows(b, 0 if reverse else CHUNK - 1)
        b_mid = _chunk_rows(b, CHUNK - 1 - CHUNK // 2 if reverse else CHUNK // 2)
        qe = q * jnp.exp(b - b_mid)
        ke = k * jnp.exp(b_mid - b)
        kl = k * jnp.exp(b_end - b)
        qb = q * jnp.exp(b)
        decay = jnp.exp(b_end)
        causal = _causal_cat(reverse, strict=False)
        st = st_ref[dr]
        order = range(N_CHUNK - 1, -1, -1) if reverse else range(N_CHUNK)
        for c in order:
            rows = slice(c * CHUNK, (c + 1) * CHUNK)
            ke_blk = jnp.where(bm_k, _tile4(ke[rows]), 0.0)
            att = jnp.where(causal, _mm_nt(qe[rows], ke_blk), 0.0)
            v_blk = jnp.where(bm_v, _tile4(v[rows]), 0.0)
            o_ref[rows, :] = _mm(att, v_blk) + _mm_nt(qb[rows], st)
            st = st * decay[c * CHUNK:c * CHUNK + 1] + jnp.where(bm_s, _mm_tn(v[rows], kl[rows]), 0.0)
        st_ref[dr] = st


def _gla(proj, n_tiles, p):
    qkf, qkb = _proj_specs(n_tiles, COL_GLA_QK)
    vf, vb = _proj_specs(n_tiles, COL_GLA_V)
    of, ob = _out_specs(n_tiles)
    n_tok = proj.shape[0]
    consts = (p['gla_w1'], p['gla_w2'], p['gla_bias'])
    out = jax.ShapeDtypeStruct((n_tok, BRANCH_W), F32)
    return pl.pallas_call(
        _gla_kernel,
        grid=(n_tiles,),
        in_specs=[qkf, vf, qkb, vb] + [_full(c) for c in consts],
        out_specs=[of, ob],
        out_shape=[out, out],
        scratch_shapes=[pltpu.VMEM((2, BRANCH_W, GLA_QK), F32)],
        compiler_params=_cparams(),
        name="gla_scan",
    )(proj, proj, proj, proj, *consts)


def _seg_sum(x, ones_blk):
    return _mm_exact_rhs(x, ones_blk)


def _rw_prepare(r_ref, k_ref, v_ref, mu_ref, kk_ref, ones_ref, is_ctx):
    def mix(x, j):
        xs = 0.5 * (_shift_rows(x, -1, is_ctx) + _shift_rows(x, 1, is_ctx))
        return x + mu_ref[j:j + 1] * (xs - x)

    r, k, v = mix(r_ref[...], 0), mix(k_ref[...], 1), mix(v_ref[...], 2)
    kk = k * kk_ref[...]
    kk = kk / jnp.maximum(jnp.sqrt(_seg_sum(kk * kk, ones_ref[...])), 1e-12)
    return r, k, v, kk


RW_GROUPS = BRANCH_W // RW_GROUP
RW_ITEMS = 2 * RW_GROUPS * N_CHUNK
OP_A_T, OP_R_T, OP_B_T, OP_K_T, OP_A_TRUE, OP_B_HAT, OP_K_HAT, OP_V = range(8)


def _dot(a, b):
    return jnp.dot(a, b, preferred_element_type=F32)


def _rwkv_kernel(rf_ref, kf_ref, vf_ref, rb_ref, kb_ref, vb_ref,
                 mu_ref, kkw_ref, ka_ref, rk_ref, ones_ref,
                 w0_ref, w1_ref, w2_ref, a0_ref, a1_ref, a2_ref, g1_ref, g2_ref,
                 yf_ref, yb_ref, bonus_ref, gate_ref,
                 zt_ref, ops_ref, rtrue_ref, gam_ref, l_ref, x_ref, gak_ref, grb_ref, grk_ref,
                 rw_ref, yloc_ref, uloc_ref):
    is_ctx = pl.program_id(0) == 0

    @pl.when(is_ctx)
    def _():
        zt_ref[...] = jnp.zeros_like(zt_ref)

    streams = ((rf_ref, kf_ref, vf_ref, yf_ref), (rb_ref, kb_ref, vb_ref, yb_ref))
    for dr, (r_ref, k_ref, v_ref, _) in enumerate(streams):
        reverse = dr == 1
        r, k, v, kk = _rw_prepare(r_ref, k_ref, v_ref, mu_ref, kkw_ref, ones_ref, is_ctx)
        if dr == 0:
            bonus_ref[...] = _seg_sum(r * k * rk_ref[...], ones_ref[...]) * v
            gate_ref[...] = _mm(_sigmoid(_mm(r, g1_ref[...])), g2_ref[...])
        lw = -RW_DECAY_SCALE * _sigmoid(w0_ref[dr:dr + 1] + _mm_hp(jnp.tanh(_mm_hp(k, w1_ref[dr])), w2_ref[dr]))
        ag = _sigmoid(a0_ref[dr:dr + 1] + _mm_hp(_mm_hp(k, a1_ref[dr]), a2_ref[dr]))
        kd = k * (1.0 + (ag - 1.0) * ka_ref[...])
        a = -kk
        b = kk * ag
        g = _seg_cumsum(lw, reverse)
        g_prev = g - lw
        g_end = _chunk_rows(g, 0 if reverse else CHUNK - 1)
        g_mid = _chunk_rows(g, CHUNK - 1 - CHUNK // 2 if reverse else CHUNK // 2)
        e_mid = jnp.exp(g_mid - g)
        e_end = jnp.exp(g_end - g)
        ops_ref[dr, OP_A_T] = _bf(a * jnp.exp(g_prev - g_mid))
        ops_ref[dr, OP_R_T] = _bf(r * jnp.exp(g - g_mid))
        ops_ref[dr, OP_B_T] = _bf(b * e_mid)
        ops_ref[dr, OP_K_T] = _bf(kd * e_mid)
        ops_ref[dr, OP_A_TRUE] = _bf(a * jnp.exp(g_prev))
        ops_ref[dr, OP_B_HAT] = _bf(b * e_end)
        ops_ref[dr, OP_K_HAT] = _bf(kd * e_end)
        ops_ref[dr, OP_V] = _bf(v)
        rtrue_ref[dr] = r * jnp.exp(g)
        gam_ref[dr] = jnp.exp(g_end)

    bm = ones_ref[0:RW_GROUP, 0:RW_GROUP]
    eye_cat = (_iota((CHUNK, RW_GROUP), 0) == (_iota((CHUNK, RW_GROUP), 1) & (CHUNK - 1))).astype(F32)

    def blk(x):
        return _tile4(x) * bm

    items = [(dr, grp, c) for dr in range(2) for grp in range(RW_GROUPS) for c in range(N_CHUNK)]

    def sl(c, grp):
        return slice(c * CHUNK, (c + 1) * CHUNK), slice(grp * RW_GROUP, (grp + 1) * RW_GROUP)

    for i, (dr, grp, c) in enumerate(items):
        rows, cols = sl(c, grp)
        lhs = jnp.concatenate([ops_ref[dr, OP_A_T, rows, cols], ops_ref[dr, OP_R_T, rows, cols]], axis=0)
        rhs = jnp.concatenate([blk(ops_ref[dr, OP_B_T, rows, cols]), blk(ops_ref[dr, OP_K_T, rows, cols])], axis=0)
        g = lax.dot_general(lhs, rhs, (((1,), (1,)), ((), ())), preferred_element_type=F32)
        strict = _causal_cat(dr == 1, strict=True)
        incl = _causal_cat(dr == 1, strict=False)
        g_ab = jnp.where(strict, g[:CHUNK, :RW_GROUP], 0.0)
        l_ref[i] = g_ab
        x_ref[i] = eye_cat + g_ab
        gak_ref[i] = _bf(jnp.where(strict, g[:CHUNK, RW_GROUP:], 0.0))
        grb_ref[i] = _bf(jnp.where(incl, g[CHUNK:, :RW_GROUP], 0.0))
        grk_ref[i] = _bf(jnp.where(incl, g[CHUNK:, RW_GROUP:], 0.0))

    n_sq = int(math.log2(CHUNK)) - 1
    for i in range(RW_ITEMS):
        lp = _bf(l_ref[i])
        l_ref[i] = _dot(lp, blk(lp))
    for step in range(1, n_sq):
        for i in range(RW_ITEMS):
            lp = _bf(l_ref[i])
            x = x_ref[i]
            res = _dot(jnp.concatenate([_bf(x), lp], axis=0), blk(lp))
            x_ref[i] = x + res[:CHUNK]
            l_ref[i] = res[CHUNK:]
    for i in range(RW_ITEMS):
        x = x_ref[i]
        x_ref[i] = x + _dot(_bf(x), blk(_bf(l_ref[i])))

    for i, (dr, grp, c) in enumerate(items):
        rows, cols = sl(c, grp)
        gak_ref[i] = _bf(_dot(gak_ref[i], blk(ops_ref[dr, OP_V, rows, cols])))

    for i, (dr, grp, c) in enumerate(items):
        rows, cols = sl(c, grp)
        rhs = jnp.concatenate([blk(ops_ref[dr, OP_A_TRUE, rows, cols]), blk(gak_ref[i])], axis=1)
        wu = _dot(_bf(x_ref[i]), rhs)
        rw_ref[i, CHUNK:, :] = _bf(wu[:, :RW_GROUP])
        uloc_ref[i] = wu[:, RW_GROUP:]

    for i, (dr, grp, c) in enumerate(items):
        rows, cols = sl(c, grp)
        rhs = jnp.concatenate([blk(rw_ref[i, CHUNK:, :]), blk(_bf(uloc_ref[i]))], axis=1)
        gwu = _dot(grb_ref[i], rhs)
        gkv = _dot(grk_ref[i], blk(ops_ref[dr, OP_V, rows, cols]))
        rw_ref[i, :CHUNK, :] = _bf(rtrue_ref[dr, rows, cols] + gwu[:, :RW_GROUP])
        yloc_ref[i] = gwu[:, RW_GROUP:] + gkv

    bm_f = bm.astype(F32)
    for step in range(N_CHUNK):
        for dr in range(2):
            c = N_CHUNK - 1 - step if dr == 1 else step
            y_ref = streams[dr][3]
            for grp in range(RW_GROUPS):
                i = (dr * RW_GROUPS + grp) * N_CHUNK + c
                rows, cols = sl(c, grp)
                zt = zt_ref[dr, grp]
                rw = lax.dot_general(rw_ref[i], _bf(zt), (((1,), (1,)), ((), ())), preferred_element_type=F32)
                y_ref[rows, cols] = rw[:CHUNK] + yloc_ref[i]
                u = rw[CHUNK:] + uloc_ref[i]
                uv = jnp.concatenate([_bf(u), ops_ref[dr, OP_V, rows, cols]], axis=0)
                bk = jnp.concatenate([ops_ref[dr, OP_B_HAT, rows, cols], ops_ref[dr, OP_K_HAT, rows, cols]], axis=0)
                upd = lax.dot_general(uv, bk, (((0,), (0,)), ((), ())), preferred_element_type=F32)
                zt_ref[dr, grp] = zt * gam_ref[dr, c * CHUNK:c * CHUNK + 1, cols] + upd * bm_f


def _rwkv(proj, n_tiles, p):
    rf, rb = _proj_specs(n_tiles, COL_RW_R)
    kf, kb = _proj_specs(n_tiles, COL_RW_K)
    vf, vb = _proj_specs(n_tiles, COL_RW_V)
    of, ob = _out_specs(n_tiles)
    n_tok = proj.shape[0]
    consts = (p['rw_mu'], p['rw_k_k'], p['rw_k_a'], p['rw_r_k'], p['ones64'],
              p['rw_w0'], p['rw_w1'], p['rw_w2'], p['rw_a0'], p['rw_a1'], p['rw_a2'], p['rw_g1'], p['rw_g2'])
    out = jax.ShapeDtypeStruct((n_tok, BRANCH_W), F32)
    return pl.pallas_call(
        _rwkv_kernel,
        grid=(n_tiles,),
        in_specs=[rf, kf, vf, rb, kb, vb] + [_full(c) for c in consts],
        out_specs=[of, ob, of, of],
        out_shape=[out, out, out, out],
        scratch_shapes=[
            pltpu.VMEM((2, RW_GROUPS, RW_GROUP, RW_GROUP), F32),
            pltpu.VMEM((2, 8, TILE, BRANCH_W), BF16),
            pltpu.VMEM((2, TILE, BRANCH_W), F32),
            pltpu.VMEM((2, TILE, BRANCH_W), F32),
            pltpu.VMEM((RW_ITEMS, CHUNK, RW_GROUP), F32),
            pltpu.VMEM((RW_ITEMS, CHUNK, RW_GROUP), F32),
            pltpu.VMEM((RW_ITEMS, CHUNK, RW_GROUP), BF16),
            pltpu.VMEM((RW_ITEMS, CHUNK, RW_GROUP), BF16),
            pltpu.VMEM((RW_ITEMS, CHUNK, RW_GROUP), BF16),
            pltpu.VMEM((RW_ITEMS, 2 * CHUNK, RW_GROUP), BF16),
            pltpu.VMEM((RW_ITEMS, CHUNK, RW_GROUP), F32),
            pltpu.VMEM((RW_ITEMS, CHUNK, RW_GROUP), F32),
        ],
        compiler_params=_cparams(),
        name="rwkv_scan",
    )(proj, proj, proj, proj, proj, proj, *consts)


def _merge_kernel(x_ref, mod_ref, u_ref, gg_ref, lg_ref,
                  s5f_ref, s5b_ref, glf_ref, glb_ref, lrf_ref, lrb_ref, rwf_ref, rwb_ref, bonus_ref, rgate_ref,
                  s5d_ref, glu_ref, gnw_ref, ones64_ref, ones128_ref, lnw_ref, lnb_ref,
                  wg_ref, bg_ref, wb_ref, wo_ref, ln1g_ref, ln1b_ref, o_ref, *, dn_alpha):
    m = _mod_rows(mod_ref, pl.program_id(0) == 0)
    x = x_ref[...]
    h = _bf(x * (1.0 + _mod_part(m, 1)) + _mod_part(m, 0))

    ya = _gelu(s5f_ref[...] + s5b_ref[...] + s5d_ref[...] * u_ref[...])
    ya = ya * _sigmoid(_mm(ya, glu_ref[...]))

    o = glf_ref[...] + glb_ref[...]
    ms = _seg_sum(o * o, ones128_ref[...]) * (1.0 / GLA_DV)
    yb = (o * lax.rsqrt(ms + LN_EPS)) * gnw_ref[...] * _silu(gg_ref[...])

    yc = (lrf_ref[...] + lrb_ref[...]) * _gelu(lg_ref[...])

    y = rwf_ref[...] + rwb_ref[...]
    mu = _seg_sum(y, ones64_ref[...]) * (1.0 / RW_N)
    yc0 = y - mu
    var = _seg_sum(yc0 * yc0, ones64_ref[...]) * (1.0 / RW_N)
    yn = (yc0 * lax.rsqrt(var + RW_GN_EPS)) * lnw_ref[...] + lnb_ref[...]
    yd = (yn + bonus_ref[...]) * rgate_ref[...]

    z = jnp.zeros((TILE, D_MODEL), F32)
    for n, yn_ in enumerate((ya, yb, yc, yd)):
        cols = slice(n * D_MODEL, (n + 1) * D_MODEL)
        gate = _sigmoid(jnp.dot(h, wg_ref[:, cols], preferred_element_type=F32) + bg_ref[:, cols])
        z = z + gate * _mm(yn_, wb_ref[n])
    out = _mm(z, wo_ref[...])
    x1 = dn_alpha * x + _mod_part(m, 2) * out
    o_ref[...] = _layer_norm(x1, LN_EPS) * ln1g_ref[...] + ln1b_ref[...]


def _merge(xs, mod, proj, parts, n_tiles, p, dn_alpha):
    n_tok = xs.shape[0]
    tile_d = pl.BlockSpec((TILE, D_MODEL), lambda i: (i, 0))
    tile_w = pl.BlockSpec((TILE, BRANCH_W), lambda i: (i, 0))

    def pcol(col):
        return pl.BlockSpec((TILE, BRANCH_W), lambda i: (i, col))

    consts = (p['s5_d'], p['s5_w_glu'], p['gla_norm_w'], p['ones64'], p['ones128'], p['rw_ln_w'], p['rw_ln_b'],
              p['w_gate'], p['b_gate'], p['w_branch'], p['w_out'], p['ln1_g'], p['ln1_b'])
    return pl.pallas_call(
        functools.partial(_merge_kernel, dn_alpha=dn_alpha),
        grid=(n_tiles,),
        in_specs=[tile_d, _full(mod), pcol(COL_S5_U), pcol(COL_GLA_G), pcol(COL_LRU_GATE)]
                 + [tile_w] * len(parts) + [_full(c) for c in consts],
        out_specs=tile_d,
        out_shape=jax.ShapeDtypeStruct((n_tok, D_MODEL), F32),
        compiler_params=_cparams(),
        name="merge_ln1",
    )(xs, mod, proj, proj, proj, *parts, *consts)


def _ffn_kernel(x_ref, mod_ref, wup_ref, cw_ref, cb_ref, wdn_ref, ln2g_ref, ln2b_ref, o_ref, *, dn_alpha):
    is_ctx = pl.program_id(0) == 0
    m = _mod_rows(mod_ref, is_ctx)
    x = x_ref[...]
    h = _bf(x * (1.0 + _mod_part(m, 4)) + _mod_part(m, 3))
    acc = jnp.zeros((TILE, D_MODEL), F32)
    for j in range(FFN_HIDDEN // FFN_CHUNK):
        ucols = slice(j * FFN_CHUNK, (j + 1) * FFN_CHUNK)
        vcols = slice(FFN_HIDDEN + j * FFN_CHUNK, FFN_HIDDEN + (j + 1) * FFN_CHUNK)
        u = jnp.dot(h, wup_ref[:, ucols], preferred_element_type=F32)
        v = jnp.dot(h, wup_ref[:, vcols], preferred_element_type=F32)
        uc = cb_ref[:, ucols] + sum(cw_ref[k:k + 1, ucols] * _shift_rows(u, k - 1, is_ctx) for k in range(3))
        acc = acc + _mm(_gelu(uc) * v, wdn_ref[ucols, :])
    x2 = dn_alpha * x + _mod_part(m, 5) * acc
    o_ref[...] = _layer_norm(x2, LN_EPS) * ln2g_ref[...] + ln2b_ref[...]


def _ffn(xs, mod, p, dn_alpha):
    n_tok = xs.shape[0]
    tile_d = pl.BlockSpec((TILE, D_MODEL), lambda i: (i, 0))
    consts = (p['ffn_w_up'], p['ffn_conv_w'], p['ffn_conv_b'], p['ffn_w_down'], p['ln2_g'], p['ln2_b'])
    return pl.pallas_call(
        functools.partial(_ffn_kernel, dn_alpha=dn_alpha),
        grid=(n_tok // TILE,),
        in_specs=[tile_d, _full(mod)] + [_full(c) for c in consts],
        out_specs=tile_d,
        out_shape=jax.ShapeDtypeStruct((n_tok, D_MODEL), F32),
        compiler_params=_cparams(),
        name="convffn_ln2",
    )(xs, mod, *consts)


def _block_diag(blocks):
    n, a, b = blocks.shape
    eye = jnp.eye(n, dtype=blocks.dtype)
    return (eye[:, None, :, None] * blocks[:, :, None, :]).reshape(n * a, n * b)


def _pad_cols(w, n):
    return jnp.pad(w, [(0, 0)] * (w.ndim - 1) + [(0, n - w.shape[-1])])


def _pad_rows(w, n):
    return jnp.pad(w, [(0, 0)] * (w.ndim - 2) + [(0, n - w.shape[-2]), (0, 0)])


def _s5_perms():
    perm = np.zeros((2, TILE, TILE), np.float32)
    for j in range(S5_STEPS):
        for s in range(S5_SUB):
            t = s * S5_STEPS + j
            perm[0, j * S5_SUB + s, t] = 1.0
            perm[1, j * S5_SUB + s, TILE - 1 - t] = 1.0
    return jnp.asarray(perm, BF16), jnp.asarray(np.transpose(perm, (0, 2, 1)), BF16)


def _ones_blocks(width):
    idx = np.arange(BRANCH_W) // width
    return jnp.asarray(idx[:, None] == idx[None, :], BF16)


def _layer_params(q, i):
    p = {}
    row = lambda v: v[i].reshape(1, -1)
    dt = jnp.exp(q['s5_log_step'][i])[:, :, None]
    lam_re, lam_im = q['s5_lam_re'][i], q['s5_lam_im'][i]
    mag = jnp.exp(lam_re * dt)
    a_re, a_im = mag * jnp.cos(lam_im * dt), mag * jnp.sin(lam_im * dt)
    den = lam_re * lam_re + lam_im * lam_im
    f_re = ((a_re - 1.0) * lam_re + a_im * lam_im) / den
    f_im = (a_im * lam_re - (a_re - 1.0) * lam_im) / den
    b_re, b_im = q['s5_b_re'][i], q['s5_b_im'][i]
    d_re = f_re[..., None] * b_re[None] - f_im[..., None] * b_im[None]
    d_im = f_re[..., None] * b_im[None] + f_im[..., None] * b_re[None]
    to_drive = lambda d: jnp.stack([_block_diag(jnp.swapaxes(d[dr], 1, 2)) for dr in range(2)])
    p['s5_drive'] = jnp.concatenate([to_drive(d_re), to_drive(d_im)], axis=-1).astype(BF16)
    p['s5_a'] = jnp.concatenate([a_re.reshape(2, 1, -1), a_im.reshape(2, 1, -1)], axis=-1)
    c_re, c_im = q['s5_c_re'][i], q['s5_c_im'][i]
    p['s5_read'] = jnp.concatenate([_block_diag(jnp.swapaxes(c_re, 1, 2)),
                                    -_block_diag(jnp.swapaxes(c_im, 1, 2))], axis=0).astype(BF16)
    p['s5_perm'], p['s5_perm_t'] = _s5_perms()
    p['s5_d'] = row(q['s5_d'])
    p['s5_w_glu'] = q['s5_w_glu'][i].astype(BF16)
    p['gla_w1'] = _pad_cols(q['gla_w1'][i], LORA_PAD)
    p['gla_w2'] = _pad_rows(q['gla_w2'][i], LORA_PAD)
    p['gla_bias'] = q['gla_bias'][i]
    p['gla_norm_w'] = jnp.tile(q['gla_norm_w'][i], GLA_HEADS).reshape(1, -1)
    p['lru_conv_w'] = q['lru_conv_w'][i]
    p['lru_conv_b'] = row(q['lru_conv_b'])
    p['lru_wa'] = jnp.stack([_block_diag(q['lru_w_a'][i, dr]) for dr in range(2)]).astype(BF16)
    p['lru_wx'] = jnp.stack([_block_diag(q['lru_w_x'][i, dr]) for dr in range(2)]).astype(BF16)
    p['lru_b_a'], p['lru_b_x'] = q['lru_b_a'][i], q['lru_b_x'][i]
    p['lru_c'] = -LRU_C * jax.nn.softplus(-q['lru_lam'][i])
    p['rw_mu'] = q['rw_mu'][i]
    for name in ('rw_k_k', 'rw_k_a', 'rw_r_k', 'rw_ln_w', 'rw_ln_b'):
        p[name] = row(q[name])
    p['rw_w0'], p['rw_a0'] = q['rw_w0'][i], q['rw_a0'][i]
    p['rw_w1'] = _pad_cols(q['rw_w1'][i], LORA_PAD)
    p['rw_w2'] = _pad_rows(q['rw_w2'][i], LORA_PAD)
    p['rw_a1'] = _pad_cols(q['rw_a1'][i], LORA_PAD)
    p['rw_a2'] = _pad_rows(q['rw_a2'][i], LORA_PAD)
    p['rw_g1'] = q['rw_g1'][i].astype(BF16)
    p['rw_g2'] = q['rw_g2'][i].astype(BF16)
    p['ones64'] = _ones_blocks(RW_N)
    p['ones128'] = _ones_blocks(GLA_DV)
    p['w_in'] = q['w_in'][i].astype(BF16)
    p['w_gate'] = q['w_gate'][i].astype(BF16)
    p['b_gate'] = row(q['b_gate'])
    p['w_branch'] = q['w_branch'][i].astype(BF16)
    p['w_out'] = q['w_out'][i].astype(BF16)
    p['ffn_w_up'] = q['ffn_w_up'][i].astype(BF16)
    p['ffn_w_down'] = q['ffn_w_down'][i].astype(BF16)
    p['ffn_conv_w'] = q['ffn_conv_w'][i]
    p['ffn_conv_b'] = row(q['ffn_conv_b'])
    for name in ('ln1_g', 'ln1_b', 'ln2_g', 'ln2_b'):
        p[name] = row(q[name])
    return p


def kernel(x, c, ctx, c_ctx, w_mod, b_mod, w_in, s5_lam_re, s5_lam_im, s5_log_step, s5_b_re, s5_b_im, s5_c_re, s5_c_im, s5_d, s5_w_glu, gla_w1, gla_w2, gla_bias, gla_norm_w, lru_conv_w, lru_conv_b, lru_w_a, lru_b_a, lru_w_x, lru_b_x, lru_lam, rw_mu, rw_w0, rw_w1, rw_w2, rw_a0, rw_a1, rw_a2, rw_g1, rw_g2, rw_k_k, rw_k_a, rw_r_k, rw_ln_w, rw_ln_b, w_branch, w_gate, b_gate, w_out, ln1_g, ln1_b, ln2_g, ln2_b, ffn_w_up, ffn_conv_w, ffn_conv_b, ffn_w_down):
    q = dict(w_in=w_in, s5_lam_re=s5_lam_re, s5_lam_im=s5_lam_im, s5_log_step=s5_log_step, s5_b_re=s5_b_re,
             s5_b_im=s5_b_im, s5_c_re=s5_c_re, s5_c_im=s5_c_im, s5_d=s5_d, s5_w_glu=s5_w_glu,
             gla_w1=gla_w1, gla_w2=gla_w2, gla_bias=gla_bias, gla_norm_w=gla_norm_w,
             lru_conv_w=lru_conv_w, lru_conv_b=lru_conv_b, lru_w_a=lru_w_a, lru_b_a=lru_b_a,
             lru_w_x=lru_w_x, lru_b_x=lru_b_x, lru_lam=lru_lam,
             rw_mu=rw_mu, rw_w0=rw_w0, rw_w1=rw_w1, rw_w2=rw_w2, rw_a0=rw_a0, rw_a1=rw_a1, rw_a2=rw_a2,
             rw_g1=rw_g1, rw_g2=rw_g2, rw_k_k=rw_k_k, rw_k_a=rw_k_a, rw_r_k=rw_r_k,
             rw_ln_w=rw_ln_w, rw_ln_b=rw_ln_b,
             w_branch=w_branch, w_gate=w_gate, b_gate=b_gate, w_out=w_out,
             ln1_g=ln1_g, ln1_b=ln1_b, ln2_g=ln2_g, ln2_b=ln2_b,
             ffn_w_up=ffn_w_up, ffn_conv_w=ffn_conv_w, ffn_conv_b=ffn_conv_b, ffn_w_down=ffn_w_down)
    batch, seq, d = x.shape
    assert batch == 1 and d == D_MODEL and ctx.shape[1] == CTX_LEN and seq % TILE == 0
    depth = w_mod.shape[0]
    dn_alpha = (2 * depth) ** 0.25
    n_tiles = 1 + seq // TILE

    cc = jnp.zeros((8, D_MODEL), F32).at[0].set(c[0]).at[1].set(c_ctx)
    mods = _modulation(cc, w_mod, b_mod)
    xs = _ln0(jnp.concatenate([ctx[0], x[0]], axis=0))
    for i in range(depth):
        p = _layer_params(q, i)
        mod = mods[i]
        proj = _inproj(xs, mod, p['w_in'])
        s5f, s5b = _s5(proj, n_tiles, p)
        glf, glb = _gla(proj, n_tiles, p)
        lrf, lrb = _lru(proj, n_tiles, p)
        rwf, rwb, bonus, rgate = _rwkv(proj, n_tiles, p)
        xs = _merge(xs, mod, proj, (s5f, s5b, glf, glb, lrf, lrb, rwf, rwb, bonus, rgate), n_tiles, p, dn_alpha)
        xs = _ffn(xs, mod, p, dn_alpha)
    return xs[TILE:].reshape(batch, seq, d)
```

```python
import functools
import math

import numpy as np
import jax
import jax.numpy as jnp
from jax import lax
from jax.experimental import pallas as pl
from jax.experimental.pallas import tpu as pltpu

F32 = jnp.float32
BF16 = jnp.bfloat16

D_MODEL = 1024
GRID_W = 64
CTX_LEN = 256
TILE = CTX_LEN
BRANCH_W = 512
N_BRANCH = 4

S5_H, S5_P = 16, 64
S5_G = BRANCH_W // S5_H
S5_STATE = S5_G * S5_P
S5_SUB = 8
S5_STEPS = TILE // S5_SUB
S5_UNROLL = 4

GLA_HEADS, GLA_DK = 4, 64
GLA_DV = BRANCH_W // GLA_HEADS
GLA_QK = GLA_HEADS * GLA_DK
GLA_TAU = 16.0
CHUNK = 64
N_CHUNK = TILE // CHUNK

LRU_C = 8.0
LRU_CONV_LEFT = 2

RW_N = 64
RW_GROUP = 256
RW_GN_EPS = 64e-5
RW_DECAY_SCALE = math.exp(-0.5)

FFN_HIDDEN = 2816
FFN_CHUNKS = (1024, 1024, 768)
assert sum(FFN_CHUNKS) == FFN_HIDDEN
LORA_PAD = 128

LN_EPS = 1e-6
SUBLANES = 8
V7X_VMEM_LIMIT_BYTES = 56 * 1024 * 1024


def _bf(x):
    return x.astype(BF16)


def _mm(a, b):
    return jnp.dot(_bf(a), _bf(b), preferred_element_type=F32)


def _dot(a, b):
    return jnp.dot(a, b, preferred_element_type=F32)


def _mm_nt(a, b):
    return lax.dot_general(_bf(a), _bf(b), (((1,), (1,)), ((), ())), preferred_element_type=F32)


def _mm_tn(a, b):
    return lax.dot_general(_bf(a), _bf(b), (((0,), (0,)), ((), ())), preferred_element_type=F32)


def _split2(x):
    hi = _bf(x)
    lo = _bf(x - hi.astype(F32))
    return hi, lo


def _split3(x):
    hi = _bf(x)
    r = x - hi.astype(F32)
    mid = _bf(r)
    lo = _bf(r - mid.astype(F32))
    return hi, mid, lo


def _mm_hp(a, b):
    ah, al = _split2(a)
    bh, bl = _split2(b)
    dot = functools.partial(jnp.dot, preferred_element_type=F32)
    return dot(ah, bh) + (dot(ah, bl) + dot(al, bh))


def _mm_exact_rhs(a, b_exact):
    ah, al = _split2(a)
    dot = functools.partial(jnp.dot, preferred_element_type=F32)
    return dot(ah, b_exact) + dot(al, b_exact)


def _seg_sum(x, ones_blk):
    return _mm_exact_rhs(x, ones_blk)


def _iota(shape, dim):
    return lax.broadcasted_iota(jnp.int32, shape, dim)


def _sigmoid(x):
    return 0.5 * jnp.tanh(0.5 * x) + 0.5


def _silu(x):
    return x * _sigmoid(x)


def _gelu(x):
    return 0.5 * x * (1.0 + jnp.tanh(math.sqrt(2.0 / math.pi) * (x + 0.044715 * (x * x * x))))


def _log_sigmoid(x):
    return jnp.minimum(x, 0.0) - jnp.log1p(jnp.exp(-jnp.abs(x)))


def _neg_expm1_twice(y):
    t = jnp.tanh(y)
    return (-2.0 * t) / (1.0 - t)


def _layer_norm(x, eps):
    mu = jnp.mean(x, axis=-1, keepdims=True)
    xc = x - mu
    var = jnp.mean(xc * xc, axis=-1, keepdims=True)
    return xc * lax.rsqrt(var + eps)


def _shift_masks(shape, shifts, is_ctx):
    n = shape[0]
    t = _iota(shape, 0)
    pos = jnp.where(is_ctx, t, t & (GRID_W - 1))
    row_len = jnp.where(is_ctx, n, GRID_W)
    return {k: ((pos + k >= 0) & (pos + k < row_len)).astype(F32) for k in shifts}


def _shift_rows(x, k, masks):
    return pltpu.roll(x, (-k) % x.shape[0], axis=0) * masks[k]


def _seg_cumsum(x, reverse):
    n = x.shape[0]
    t = _iota(x.shape, 0) & (CHUNK - 1)
    d = 1
    while d < CHUNK:
        if reverse:
            x = x + jnp.where(t < CHUNK - d, pltpu.roll(x, n - d, axis=0), 0.0)
        else:
            x = x + jnp.where(t >= d, pltpu.roll(x, d, axis=0), 0.0)
        d *= 2
    return x


def _scan_tile(a, b, carry, reverse):
    n = a.shape[0]
    t = _iota(a.shape, 0) & (SUBLANES - 1)
    d = 1
    while d < SUBLANES:
        if reverse:
            valid, sh = t < SUBLANES - d, n - d
        else:
            valid, sh = t >= d, d
        a_s = jnp.where(valid, pltpu.roll(a, sh, axis=0), 1.0)
        b_s = jnp.where(valid, pltpu.roll(b, sh, axis=0), 0.0)
        b = b + a * b_s
        a = a * a_s
        d *= 2
    n_grp = n // SUBLANES
    last = 0 if reverse else SUBLANES - 1
    parts = [None] * n_grp
    for gi in (range(n_grp - 1, -1, -1) if reverse else range(n_grp)):
        rows = slice(gi * SUBLANES, (gi + 1) * SUBLANES)
        h = b[rows] + a[rows] * carry
        parts[gi] = h
        carry = h[last:last + 1]
    return jnp.concatenate(parts, axis=0), carry


def _chunk_rows(x, row):
    parts = [jnp.broadcast_to(x[c * CHUNK + row:c * CHUNK + row + 1], (CHUNK, x.shape[1]))
             for c in range(x.shape[0] // CHUNK)]
    return jnp.concatenate(parts, axis=0)


def _tile4(x):
    return jnp.concatenate([x] * (RW_GROUP // CHUNK), axis=0)


def _block_mask(rows, cols, row_blk, col_blk):
    return (_iota((rows, cols), 0) // row_blk) == (_iota((rows, cols), 1) // col_blk)


def _causal_cat(reverse, strict):
    t = _iota((CHUNK, RW_GROUP), 0)
    s = _iota((CHUNK, RW_GROUP), 1) & (CHUNK - 1)
    if reverse:
        return (s > t) if strict else (s >= t)
    return (s < t) if strict else (s <= t)


def _bwd_tile(i, n_tiles):
    return jnp.where(i == 0, 0, n_tiles - i)


def _cparams():
    return pltpu.CompilerParams(dimension_semantics=("arbitrary",), vmem_limit_bytes=V7X_VMEM_LIMIT_BYTES)


def _full(arr):
    nd = arr.ndim
    return pl.BlockSpec(arr.shape, lambda i, _nd=nd: (0,) * _nd)


def _ln0_kernel(x_ref, o_ref):
    o_ref[...] = _layer_norm(x_ref[...], LN_EPS)


def _ln0(xcat):
    n_tok = xcat.shape[0]
    return pl.pallas_call(
        _ln0_kernel,
        grid=(n_tok // TILE,),
        in_specs=[pl.BlockSpec((TILE, D_MODEL), lambda i: (i, 0))],
        out_specs=pl.BlockSpec((TILE, D_MODEL), lambda i: (i, 0)),
        out_shape=jax.ShapeDtypeStruct((n_tok, D_MODEL), F32),
        compiler_params=_cparams(),
        name="ln0",
    )(xcat)


MOD_COLS = 1536


def _mod_kernel(c_ref, w_ref, b_ref, o_ref):
    s = _silu(c_ref[...])
    o_ref[0] = _mm_hp(s, w_ref[0]) + b_ref[0]


def _modulation(cc, w_mod, b_mod):
    depth = w_mod.shape[0]
    n_col = w_mod.shape[2] // MOD_COLS
    return pl.pallas_call(
        _mod_kernel,
        grid=(depth, n_col),
        in_specs=[pl.BlockSpec((8, D_MODEL), lambda l, j: (0, 0)),
                  pl.BlockSpec((1, D_MODEL, MOD_COLS), lambda l, j: (l, 0, j)),
                  pl.BlockSpec((1, 1, MOD_COLS), lambda l, j: (l, 0, j))],
        out_specs=pl.BlockSpec((1, 8, MOD_COLS), lambda l, j: (l, 0, j)),
        out_shape=jax.ShapeDtypeStruct((depth, 8, w_mod.shape[2]), F32),
        compiler_params=pltpu.CompilerParams(dimension_semantics=("arbitrary", "arbitrary"),
                                             vmem_limit_bytes=V7X_VMEM_LIMIT_BYTES),
        name="modulation",
    )(cc, w_mod, b_mod.reshape(depth, 1, -1))


def _mod_rows(mod_ref, is_ctx):
    m = mod_ref[...]
    return jnp.where(is_ctx, m[1:2], m[0:1])


def _mod_part(m, j):
    return m[:, j * D_MODEL:(j + 1) * D_MODEL]


(COL_S5_U, COL_GLA_QK, COL_GLA_V, COL_GLA_G, COL_LRU_X, COL_LRU_GATE, COL_RW_R, COL_RW_K, COL_RW_V,
 COL_RW_KK) = range(10)
N_PROJ_COLS = 10 * BRANCH_W


def _col(block):
    return slice(block * BRANCH_W, (block + 1) * BRANCH_W)


def _inproj_kernel(x_ref, mod_ref, w_ref, cw_ref, cb_ref, mu_ref, kkw_ref, ones_ref, o_ref):
    is_ctx = pl.program_id(0) == 0
    m = _mod_rows(mod_ref, is_ctx)
    h = _bf(x_ref[...] * (1.0 + _mod_part(m, 1)) + _mod_part(m, 0))
    n_taps = cw_ref.shape[0]
    taps = [k - LRU_CONV_LEFT for k in range(n_taps)]
    masks = _shift_masks((TILE, BRANCH_W), sorted(set(taps + [-1, 1]) - {0}), is_ctx)

    x = _dot(h, w_ref[:, _col(COL_LRU_X)])
    o_ref[:, _col(COL_LRU_X)] = cb_ref[...] + sum(
        cw_ref[j:j + 1] * (_shift_rows(x, k, masks) if k else x) for j, k in enumerate(taps))

    for j, block in enumerate((COL_RW_R, COL_RW_K, COL_RW_V)):
        x = _dot(h, w_ref[:, _col(block)])
        xs = 0.5 * (_shift_rows(x, -1, masks) + _shift_rows(x, 1, masks))
        x = x + mu_ref[j:j + 1] * (xs - x)
        o_ref[:, _col(block)] = x
        if block == COL_RW_K:
            kk = x * kkw_ref[...]
            o_ref[:, _col(COL_RW_KK)] = kk / jnp.maximum(jnp.sqrt(_seg_sum(kk * kk, ones_ref[...])), 1e-12)

    plain = slice(0, COL_LRU_X * BRANCH_W)
    o_ref[:, plain] = _dot(h, w_ref[:, plain])
    o_ref[:, _col(COL_LRU_GATE)] = _dot(h, w_ref[:, _col(COL_LRU_GATE)])


def _inproj(xs, mod, p):
    n_tok = xs.shape[0]
    consts = (p['w_in'], p['lru_conv_w'], p['lru_conv_b'], p['rw_mu'], p['rw_k_k'], p['ones64'])
    return pl.pallas_call(
        _inproj_kernel,
        grid=(n_tok // TILE,),
        in_specs=[pl.BlockSpec((TILE, D_MODEL), lambda i: (i, 0)), _full(mod)] + [_full(c) for c in consts],
        out_specs=pl.BlockSpec((TILE, N_PROJ_COLS), lambda i: (i, 0)),
        out_shape=jax.ShapeDtypeStruct((n_tok, N_PROJ_COLS), F32),
        compiler_params=_cparams(),
        name="inproj",
    )(xs, mod, *consts)


def _proj_specs(n_tiles, col):
    fwd = pl.BlockSpec((TILE, BRANCH_W), lambda i: (i, col))
    bwd = pl.BlockSpec((TILE, BRANCH_W), lambda i: (_bwd_tile(i, n_tiles), col))
    return fwd, bwd


def _out_specs(n_tiles):
    fwd = pl.BlockSpec((TILE, BRANCH_W), lambda i: (i, 0))
    bwd = pl.BlockSpec((TILE, BRANCH_W), lambda i: (_bwd_tile(i, n_tiles), 0))
    return fwd, bwd


def _lru_kernel(xf_ref, xb_ref, wa_ref, ba_ref, wx_ref, bx_ref, lc_ref,
                hf_ref, hb_ref, carry_ref):
    @pl.when(pl.program_id(0) == 0)
    def _():
        carry_ref[...] = jnp.zeros_like(carry_ref)

    for dr, (x_ref, o_ref) in enumerate(((xf_ref, hf_ref), (xb_ref, hb_ref))):
        xc = x_ref[...]
        r = _sigmoid(_mm(xc, wa_ref[dr]) + ba_ref[dr:dr + 1])
        ig = _sigmoid(_mm(xc, wx_ref[dr]) + bx_ref[dr:dr + 1])
        log_a = lc_ref[dr:dr + 1] * r
        a = jnp.exp(log_a)
        b = jnp.sqrt(_neg_expm1_twice(log_a)) * (ig * xc)
        h, carry = _scan_tile(a, b, carry_ref[dr:dr + 1], reverse=dr == 1)
        o_ref[...] = h
        carry_ref[dr:dr + 1] = carry


def _lru(proj, n_tiles, p):
    xf, xb = _proj_specs(n_tiles, COL_LRU_X)
    of, ob = _out_specs(n_tiles)
    n_tok = proj.shape[0]
    consts = (p['lru_wa'], p['lru_b_a'], p['lru_wx'], p['lru_b_x'], p['lru_c'])
    out = jax.ShapeDtypeStruct((n_tok, BRANCH_W), F32)
    return pl.pallas_call(
        _lru_kernel,
        grid=(n_tiles,),
        in_specs=[xf, xb] + [_full(c) for c in consts],
        out_specs=[of, ob],
        out_shape=[out, out],
        scratch_shapes=[pltpu.VMEM((8, BRANCH_W), F32)],
        compiler_params=_cparams(),
        name="lru_scan",
    )(proj, proj, *consts)


S5_HALF = S5_STATE // 2


def _cmul(ar, ai, br, bi):
    return ar * br - ai * bi, ar * bi + ai * br


def _s5_cols(half):
    base = 2 * S5_HALF * half
    return slice(base, base + S5_HALF), slice(base + S5_HALF, base + 2 * S5_HALF)


def _s5_kernel(uf_ref, ub_ref, perm_ref, permt_ref, a_ref, drive_ref, read_ref,
               yf_ref, yb_ref, h_ref, carry_ref, pw_ref):
    @pl.when(pl.program_id(0) == 0)
    def _():
        carry_ref[...] = jnp.zeros_like(carry_ref)
        for dr in range(2):
            for half in range(2):
                re, im = _s5_cols(half)
                a_re, a_im = a_ref[dr, :, re], a_ref[dr, :, im]
                q_re, q_im = a_re, a_im
                for j in range(S5_STEPS):
                    rows = slice(j * S5_SUB, (j + 1) * S5_SUB)
                    pw_ref[dr, rows, re] = jnp.broadcast_to(q_re, (S5_SUB, S5_HALF))
                    pw_ref[dr, rows, im] = jnp.broadcast_to(q_im, (S5_SUB, S5_HALF))
                    q_re, q_im = _cmul(q_re, q_im, a_re, a_im)

    sub = _iota((S5_SUB, S5_HALF), 0)
    half_u = BRANCH_W // 2
    for dr, (u_ref, o_ref) in enumerate(((uf_ref, yf_ref), (ub_ref, yb_ref))):
        u_p = _bf(jnp.dot(perm_ref[dr], _bf(u_ref[...]), preferred_element_type=F32))
        for half in range(2):
            h_ref[dr, :, 2 * S5_HALF * half:2 * S5_HALF * (half + 1)] = _dot(
                u_p[:, half * half_u:(half + 1) * half_u], drive_ref[dr, half])
        for half in range(2):
            re, im = _s5_cols(half)
            a_re = a_ref[dr, :, re]
            a_im = a_ref[dr, :, im]
            a_re8 = jnp.broadcast_to(a_re, (S5_SUB, S5_HALF))
            a_im8 = jnp.broadcast_to(a_im, (S5_SUB, S5_HALF))

            def local_step(j, carry, dr=dr, re=re, im=im, a_re8=a_re8, a_im8=a_im8):
                h_re, h_im = carry
                rows = pl.ds(pl.multiple_of(j * S5_SUB, S5_SUB), S5_SUB)
                m_re, m_im = _cmul(a_re8, a_im8, h_re, h_im)
                h_re = m_re + h_ref[dr, rows, re]
                h_im = m_im + h_ref[dr, rows, im]
                h_ref[dr, rows, re] = h_re
                h_ref[dr, rows, im] = h_im
                return h_re, h_im

            zero = jnp.zeros((S5_SUB, S5_HALF), F32)
            end_re, end_im = lax.fori_loop(0, S5_STEPS, local_step, (zero, zero), unroll=S5_UNROLL)

            p_re, p_im = a_re, a_im
            for _ in range(int(math.log2(S5_STEPS))):
                p_re, p_im = _cmul(p_re, p_im, p_re, p_im)
            s_re = carry_ref[dr, :, re]
            s_im = carry_ref[dr, :, im]
            in_re = jnp.zeros((S5_SUB, S5_HALF), F32)
            in_im = jnp.zeros((S5_SUB, S5_HALF), F32)
            for s in range(S5_SUB):
                in_re = jnp.where(sub == s, s_re, in_re)
                in_im = jnp.where(sub == s, s_im, in_im)
                m_re, m_im = _cmul(p_re, p_im, s_re, s_im)
                s_re = m_re + end_re[s:s + 1]
                s_im = m_im + end_im[s:s + 1]
            carry_ref[dr, :, re] = s_re
            carry_ref[dr, :, im] = s_im

            def fix_step(j, carry, dr=dr, re=re, im=im, in_re=in_re, in_im=in_im):
                rows = pl.ds(pl.multiple_of(j * S5_SUB, S5_SUB), S5_SUB)
                c_re, c_im = _cmul(pw_ref[dr, rows, re], pw_ref[dr, rows, im], in_re, in_im)
                h_ref[dr, rows, re] = h_ref[dr, rows, re] + c_re
                h_ref[dr, rows, im] = h_ref[dr, rows, im] + c_im
                return carry

            lax.fori_loop(0, S5_STEPS, fix_step, 0, unroll=S5_UNROLL)

        y_p = jnp.concatenate(
            [_mm(h_ref[dr, :, 2 * S5_HALF * half:2 * S5_HALF * (half + 1)], read_ref[half]) for half in range(2)],
            axis=1)
        hi, mid, lo = _split3(y_p)
        dot = functools.partial(jnp.dot, preferred_element_type=F32)
        o_ref[...] = (dot(permt_ref[dr], hi) + dot(permt_ref[dr], mid)) + dot(permt_ref[dr], lo)


def _s5(proj, n_tiles, p):
    uf, ub = _proj_specs(n_tiles, COL_S5_U)
    of, ob = _out_specs(n_tiles)
    n_tok = proj.shape[0]
    consts = (p['s5_perm'], p['s5_perm_t'], p['s5_a'], p['s5_drive'], p['s5_read'])
    out = jax.ShapeDtypeStruct((n_tok, BRANCH_W), F32)
    return pl.pallas_call(
        _s5_kernel,
        grid=(n_tiles,),
        in_specs=[uf, ub] + [_full(c) for c in consts],
        out_specs=[of, ob],
        out_shape=[out, out],
        scratch_shapes=[pltpu.VMEM((2, TILE, 2 * S5_STATE), F32), pltpu.VMEM((2, 1, 2 * S5_STATE), F32),
                        pltpu.VMEM((2, TILE, 2 * S5_STATE), F32)],
        compiler_params=_cparams(),
        name="s5_scan",
    )(proj, proj, *consts)


def _gla_kernel(qkf_ref, vf_ref, qkb_ref, vb_ref, w1_ref, w2_ref, bias_ref, of_ref, ob_ref, st_ref):
    @pl.when(pl.program_id(0) == 0)
    def _():
        st_ref[...] = jnp.zeros_like(st_ref)

    bm_k = _block_mask(RW_GROUP, GLA_QK, CHUNK, GLA_DK)
    bm_v = _block_mask(RW_GROUP, BRANCH_W, CHUNK, GLA_DV)
    bm_s = _block_mask(BRANCH_W, GLA_QK, GLA_DV, GLA_DK)
    for dr, (qk_ref, v_ref, o_ref) in enumerate(((qkf_ref, vf_ref, of_ref), (qkb_ref, vb_ref, ob_ref))):
        reverse = dr == 1
        q = qk_ref[:, :GLA_QK] * (GLA_DK ** -0.5)
        k = qk_ref[:, GLA_QK:]
        v = v_ref[...]
        la = _log_sigmoid(_mm_hp(_mm_hp(k, w1_ref[dr]), w2_ref[dr]) + bias_ref[dr:dr + 1]) * (1.0 / GLA_TAU)
        b = _seg_cumsum(la, reverse)
        b_end = _chunk_rows(b, 0 if reverse else CHUNK - 1)
        b_mid = _chunk_rows(b, CHUNK - 1 - CHUNK // 2 if reverse else CHUNK // 2)
        qe = q * jnp.exp(b - b_mid)
        ke = k * jnp.exp(b_mid - b)
        kl = k * jnp.exp(b_end - b)
        qb = q * jnp.exp(b)
        decay = jnp.exp(b_end)
        causal = _causal_cat(reverse, strict=False)
        st = st_ref[dr]
        order = range(N_CHUNK - 1, -1, -1) if reverse else range(N_CHUNK)
        for c in order:
            rows = slice(c * CHUNK, (c + 1) * CHUNK)
            ke_blk = jnp.where(bm_k, _tile4(ke[rows]), 0.0)
            att = jnp.where(causal, _mm_nt(qe[rows], ke_blk), 0.0)
            v_blk = jnp.where(bm_v, _tile4(v[rows]), 0.0)
            o_ref[rows, :] = _mm(att, v_blk) + _mm_nt(qb[rows], st)
            st = st * decay[c * CHUNK:c * CHUNK + 1] + jnp.where(bm_s, _mm_tn(v[rows], kl[rows]), 0.0)
        st_ref[dr] = st


def _gla(proj, n_tiles, p):
    qkf, qkb = _proj_specs(n_tiles, COL_GLA_QK)
    vf, vb = _proj_specs(n_tiles, COL_GLA_V)
    of, ob = _out_specs(n_tiles)
    n_tok = proj.shape[0]
    consts = (p['gla_w1'], p['gla_w2'], p['gla_bias'])
    out = jax.ShapeDtypeStruct((n_tok, BRANCH_W), F32)
    return pl.pallas_call(
        _gla_kernel,
        grid=(n_tiles,),
        in_specs=[qkf, vf, qkb, vb] + [_full(c) for c in consts],
        out_specs=[of, ob],
        out_shape=[out, out],
        scratch_shapes=[pltpu.VMEM((2, BRANCH_W, GLA_QK), F32)],
        compiler_params=_cparams(),
        name="gla_scan",
    )(proj, proj, proj, proj, *consts)


RW_GROUPS = BRANCH_W // RW_GROUP
RW_ITEMS = 2 * RW_GROUPS * N_CHUNK
OP_A_T, OP_R_T, OP_B_T, OP_K_T, OP_A_TRUE, OP_B_HAT, OP_K_HAT, OP_V = range(8)


def _rwkv_kernel(rf_ref, kf_ref, vf_ref, kkf_ref, rb_ref, kb_ref, vb_ref, kkb_ref,
                 ka_ref, rk_ref, ones_ref,
                 w0_ref, w1_ref, w2_ref, a0_ref, a1_ref, a2_ref, g1_ref, g2_ref,
                 yf_ref, yb_ref, bonus_ref, gate_ref,
                 zt_ref, ops_ref, rtrue_ref, gam_ref, l_ref, x_ref, gak_ref, grb_ref, grk_ref,
                 rw_ref, yloc_ref, uloc_ref):
    is_ctx = pl.program_id(0) == 0

    @pl.when(is_ctx)
    def _():
        zt_ref[...] = jnp.zeros_like(zt_ref)

    streams = ((rf_ref, kf_ref, vf_ref, yf_ref), (rb_ref, kb_ref, vb_ref, yb_ref))
    for dr, (r_ref, k_ref, v_ref, _) in enumerate(streams):
        reverse = dr == 1
        r, k, v = r_ref[...], k_ref[...], v_ref[...]
        kk = (kkb_ref if reverse else kkf_ref)[...]
        if dr == 0:
            bonus_ref[...] = _seg_sum(r * k * rk_ref[...], ones_ref[...]) * v
            gate_ref[...] = _mm(_sigmoid(_mm(r, g1_ref[...])), g2_ref[...])
        lw = -RW_DECAY_SCALE * _sigmoid(w0_ref[dr:dr + 1] + _mm_hp(jnp.tanh(_mm_hp(k, w1_ref[dr])), w2_ref[dr]))
        ag = _sigmoid(a0_ref[dr:dr + 1] + _mm_hp(_mm_hp(k, a1_ref[dr]), a2_ref[dr]))
        kd = k * (1.0 + (ag - 1.0) * ka_ref[...])
        a = -kk
        b = kk * ag
        g = _seg_cumsum(lw, reverse)
        g_prev = g - lw
        g_end = _chunk_rows(g, 0 if reverse else CHUNK - 1)
        g_mid = _chunk_rows(g, CHUNK - 1 - CHUNK // 2 if reverse else CHUNK // 2)
        e_mid = jnp.exp(g_mid - g)
        e_end = jnp.exp(g_end - g)
        ops_ref[dr, OP_A_T] = _bf(a * jnp.exp(g_prev - g_mid))
        ops_ref[dr, OP_R_T] = _bf(r * jnp.exp(g - g_mid))
        ops_ref[dr, OP_B_T] = _bf(b * e_mid)
        ops_ref[dr, OP_K_T] = _bf(kd * e_mid)
        ops_ref[dr, OP_A_TRUE] = _bf(a * jnp.exp(g_prev))
        ops_ref[dr, OP_B_HAT] = _bf(b * e_end)
        ops_ref[dr, OP_K_HAT] = _bf(kd * e_end)
        ops_ref[dr, OP_V] = _bf(v)
        rtrue_ref[dr] = r * jnp.exp(g)
        gam_ref[dr] = jnp.exp(g_end)

    bm = ones_ref[0:RW_GROUP, 0:RW_GROUP]
    eye_cat = (_iota((CHUNK, RW_GROUP), 0) == (_iota((CHUNK, RW_GROUP), 1) & (CHUNK - 1))).astype(F32)

    def blk(x):
        return _tile4(x) * bm

    items = [(dr, grp, c) for dr in range(2) for grp in range(RW_GROUPS) for c in range(N_CHUNK)]

    def sl(c, grp):
        return slice(c * CHUNK, (c + 1) * CHUNK), slice(grp * RW_GROUP, (grp + 1) * RW_GROUP)

    for i, (dr, grp, c) in enumerate(items):
        rows, cols = sl(c, grp)
        lhs = jnp.concatenate([ops_ref[dr, OP_A_T, rows, cols], ops_ref[dr, OP_R_T, rows, cols]], axis=0)
        rhs = jnp.concatenate([blk(ops_ref[dr, OP_B_T, rows, cols]), blk(ops_ref[dr, OP_K_T, rows, cols])], axis=0)
        g = lax.dot_general(lhs, rhs, (((1,), (1,)), ((), ())), preferred_element_type=F32)
        strict = _causal_cat(dr == 1, strict=True)
        incl = _causal_cat(dr == 1, strict=False)
        g_ab = jnp.where(strict, g[:CHUNK, :RW_GROUP], 0.0)
        l_ref[i] = g_ab
        x_ref[i] = eye_cat + g_ab
        gak_ref[i] = _bf(jnp.where(strict, g[:CHUNK, RW_GROUP:], 0.0))
        grb_ref[i] = _bf(jnp.where(incl, g[CHUNK:, :RW_GROUP], 0.0))
        grk_ref[i] = _bf(jnp.where(incl, g[CHUNK:, RW_GROUP:], 0.0))

    n_sq = int(math.log2(CHUNK)) - 1
    for i in range(RW_ITEMS):
        lp = _bf(l_ref[i])
        l_ref[i] = _dot(lp, blk(lp))
    for step in range(1, n_sq):
        for i in range(RW_ITEMS):
            lp = _bf(l_ref[i])
            x = x_ref[i]
            res = _dot(jnp.concatenate([_bf(x), lp], axis=0), blk(lp))
            x_ref[i] = x + res[:CHUNK]
            l_ref[i] = res[CHUNK:]
    for i in range(RW_ITEMS):
        x = x_ref[i]
        x_ref[i] = x + _dot(_bf(x), blk(_bf(l_ref[i])))

    for i, (dr, grp, c) in enumerate(items):
        rows, cols = sl(c, grp)
        gak_ref[i] = _bf(_dot(gak_ref[i], blk(ops_ref[dr, OP_V, rows, cols])))

    for i, (dr, grp, c) in enumerate(items):
        rows, cols = sl(c, grp)
        rhs = jnp.concatenate([blk(ops_ref[dr, OP_A_TRUE, rows, cols]), blk(gak_ref[i])], axis=1)
        wu = _dot(_bf(x_ref[i]), rhs)
        rw_ref[i, CHUNK:, :] = _bf(wu[:, :RW_GROUP])
        uloc_ref[i] = wu[:, RW_GROUP:]

    for i, (dr, grp, c) in enumerate(items):
        rows, cols = sl(c, grp)
        rhs = jnp.concatenate([blk(rw_ref[i, CHUNK:, :]), blk(_bf(uloc_ref[i]))], axis=1)
        gwu = _dot(grb_ref[i], rhs)
        gkv = _dot(grk_ref[i], blk(ops_ref[dr, OP_V, rows, cols]))
        rw_ref[i, :CHUNK, :] = _bf(rtrue_ref[dr, rows, cols] + gwu[:, :RW_GROUP])
        yloc_ref[i] = gwu[:, RW_GROUP:] + gkv

    bm_f = bm.astype(F32)
    for step in range(N_CHUNK):
        for dr in range(2):
            c = N_CHUNK - 1 - step if dr == 1 else step
            y_ref = streams[dr][3]
            for grp in range(RW_GROUPS):
                i = (dr * RW_GROUPS + grp) * N_CHUNK + c
                rows, cols = sl(c, grp)
                zt = zt_ref[dr, grp]
                rw = lax.dot_general(rw_ref[i], _bf(zt), (((1,), (1,)), ((), ())), preferred_element_type=F32)
                y_ref[rows, cols] = rw[:CHUNK] + yloc_ref[i]
                u = rw[CHUNK:] + uloc_ref[i]
                uv = jnp.concatenate([_bf(u), ops_ref[dr, OP_V, rows, cols]], axis=0)
                bk = jnp.concatenate([ops_ref[dr, OP_B_HAT, rows, cols], ops_ref[dr, OP_K_HAT, rows, cols]], axis=0)
                upd = lax.dot_general(uv, bk, (((0,), (0,)), ((), ())), preferred_element_type=F32)
                zt_ref[dr, grp] = zt * gam_ref[dr, c * CHUNK:c * CHUNK + 1, cols] + upd * bm_f


def _rwkv(proj, n_tiles, p):
    rf, rb = _proj_specs(n_tiles, COL_RW_R)
    kf, kb = _proj_specs(n_tiles, COL_RW_K)
    vf, vb = _proj_specs(n_tiles, COL_RW_V)
    of, ob = _out_specs(n_tiles)
    n_tok = proj.shape[0]
    kkf, kkb = _proj_specs(n_tiles, COL_RW_KK)
    consts = (p['rw_k_a'], p['rw_r_k'], p['ones64'],
              p['rw_w0'], p['rw_w1'], p['rw_w2'], p['rw_a0'], p['rw_a1'], p['rw_a2'], p['rw_g1'], p['rw_g2'])
    out = jax.ShapeDtypeStruct((n_tok, BRANCH_W), F32)
    return pl.pallas_call(
        _rwkv_kernel,
        grid=(n_tiles,),
        in_specs=[rf, kf, vf, kkf, rb, kb, vb, kkb] + [_full(c) for c in consts],
        out_specs=[of, ob, of, of],
        out_shape=[out, out, out, out],
        scratch_shapes=[
            pltpu.VMEM((2, RW_GROUPS, RW_GROUP, RW_GROUP), F32),
            pltpu.VMEM((2, 8, TILE, BRANCH_W), BF16),
            pltpu.VMEM((2, TILE, BRANCH_W), F32),
            pltpu.VMEM((2, TILE, BRANCH_W), F32),
            pltpu.VMEM((RW_ITEMS, CHUNK, RW_GROUP), F32),
            pltpu.VMEM((RW_ITEMS, CHUNK, RW_GROUP), F32),
            pltpu.VMEM((RW_ITEMS, CHUNK, RW_GROUP), BF16),
            pltpu.VMEM((RW_ITEMS, CHUNK, RW_GROUP), BF16),
            pltpu.VMEM((RW_ITEMS, CHUNK, RW_GROUP), BF16),
            pltpu.VMEM((RW_ITEMS, 2 * CHUNK, RW_GROUP), BF16),
            pltpu.VMEM((RW_ITEMS, CHUNK, RW_GROUP), F32),
            pltpu.VMEM((RW_ITEMS, CHUNK, RW_GROUP), F32),
        ],
        compiler_params=_cparams(),
        name="rwkv_scan",
    )(proj, proj, proj, proj, proj, proj, proj, proj, *consts)


def _merge_kernel(x_ref, mod_ref, u_ref, gg_ref, lg_ref,
                  s5f_ref, s5b_ref, glf_ref, glb_ref, lrf_ref, lrb_ref, rwf_ref, rwb_ref, bonus_ref, rgate_ref,
                  s5d_ref, glu_ref, gnw_ref, ones64_ref, ones128_ref, lnw_ref, lnb_ref,
                  wg_ref, bg_ref, wb_ref, wo_ref, ln1g_ref, ln1b_ref, o_ref, *, dn_alpha):
    m = _mod_rows(mod_ref, pl.program_id(0) == 0)
    x = x_ref[...]
    h = _bf(x * (1.0 + _mod_part(m, 1)) + _mod_part(m, 0))

    ya = _gelu(s5f_ref[...] + s5b_ref[...] + s5d_ref[...] * u_ref[...])
    ya = ya * _sigmoid(_mm(ya, glu_ref[...]))

    o = glf_ref[...] + glb_ref[...]
    ms = _seg_sum(o * o, ones128_ref[...]) * (1.0 / GLA_DV)
    yb = (o * lax.rsqrt(ms + LN_EPS)) * gnw_ref[...] * _silu(gg_ref[...])

    yc = (lrf_ref[...] + lrb_ref[...]) * _gelu(lg_ref[...])

    y = rwf_ref[...] + rwb_ref[...]
    mu = _seg_sum(y, ones64_ref[...]) * (1.0 / RW_N)
    yc0 = y - mu
    var = _seg_sum(yc0 * yc0, ones64_ref[...]) * (1.0 / RW_N)
    yn = (yc0 * lax.rsqrt(var + RW_GN_EPS)) * lnw_ref[...] + lnb_ref[...]
    yd = (yn + bonus_ref[...]) * rgate_ref[...]

    z = jnp.zeros((TILE, D_MODEL), F32)
    for n, yn_ in enumerate((ya, yb, yc, yd)):
        cols = slice(n * D_MODEL, (n + 1) * D_MODEL)
        gate = _sigmoid(jnp.dot(h, wg_ref[:, cols], preferred_element_type=F32) + bg_ref[:, cols])
        z = z + gate * _mm(yn_, wb_ref[n])
    out = _mm(z, wo_ref[...])
    x1 = dn_alpha * x + _mod_part(m, 2) * out
    o_ref[...] = _layer_norm(x1, LN_EPS) * ln1g_ref[...] + ln1b_ref[...]


def _merge(xs, mod, proj, parts, n_tiles, p, dn_alpha):
    n_tok = xs.shape[0]
    tile_d = pl.BlockSpec((TILE, D_MODEL), lambda i: (i, 0))
    tile_w = pl.BlockSpec((TILE, BRANCH_W), lambda i: (i, 0))

    def pcol(col):
        return pl.BlockSpec((TILE, BRANCH_W), lambda i: (i, col))

    consts = (p['s5_d'], p['s5_w_glu'], p['gla_norm_w'], p['ones64'], p['ones128'], p['rw_ln_w'], p['rw_ln_b'],
              p['w_gate'], p['b_gate'], p['w_branch'], p['w_out'], p['ln1_g'], p['ln1_b'])
    return pl.pallas_call(
        functools.partial(_merge_kernel, dn_alpha=dn_alpha),
        grid=(n_tiles,),
        in_specs=[tile_d, _full(mod), pcol(COL_S5_U), pcol(COL_GLA_G), pcol(COL_LRU_GATE)]
                 + [tile_w] * len(parts) + [_full(c) for c in consts],
        out_specs=tile_d,
        out_shape=jax.ShapeDtypeStruct((n_tok, D_MODEL), F32),
        compiler_params=_cparams(),
        name="merge_ln1",
    )(xs, mod, proj, proj, proj, *parts, *consts)


def _ffn_kernel(x_ref, mod_ref, wup_ref, cw_ref, cb_ref, wdn_ref, ln2g_ref, ln2b_ref, o_ref, *, dn_alpha):
    is_ctx = pl.program_id(0) == 0
    m = _mod_rows(mod_ref, is_ctx)
    x = x_ref[...]
    h = _bf(x * (1.0 + _mod_part(m, 4)) + _mod_part(m, 3))
    acc = jnp.zeros((TILE, D_MODEL), F32)
    masks_by_width = {w: _shift_masks((TILE, w), (-1, 1), is_ctx) for w in set(FFN_CHUNKS)}
    start = 0
    for width in FFN_CHUNKS:
        ucols = slice(start, start + width)
        vcols = slice(FFN_HIDDEN + start, FFN_HIDDEN + start + width)
        start += width
        masks = masks_by_width[width]
        u = jnp.dot(h, wup_ref[:, ucols], preferred_element_type=F32)
        v = jnp.dot(h, wup_ref[:, vcols], preferred_element_type=F32)
        uc = (cb_ref[:, ucols] + cw_ref[1:2, ucols] * u
              + cw_ref[0:1, ucols] * _shift_rows(u, -1, masks) + cw_ref[2:3, ucols] * _shift_rows(u, 1, masks))
        acc = acc + _mm(_gelu(uc) * v, wdn_ref[ucols, :])
    x2 = dn_alpha * x + _mod_part(m, 5) * acc
    o_ref[...] = _layer_norm(x2, LN_EPS) * ln2g_ref[...] + ln2b_ref[...]


def _ffn(xs, mod, p, dn_alpha):
    n_tok = xs.shape[0]
    tile_d = pl.BlockSpec((TILE, D_MODEL), lambda i: (i, 0))
    consts = (p['ffn_w_up'], p['ffn_conv_w'], p['ffn_conv_b'], p['ffn_w_down'], p['ln2_g'], p['ln2_b'])
    return pl.pallas_call(
        functools.partial(_ffn_kernel, dn_alpha=dn_alpha),
        grid=(n_tok // TILE,),
        in_specs=[tile_d, _full(mod)] + [_full(c) for c in consts],
        out_specs=tile_d,
        out_shape=jax.ShapeDtypeStruct((n_tok, D_MODEL), F32),
        compiler_params=_cparams(),
        name="convffn_ln2",
    )(xs, mod, *consts)


def _block_diag(blocks):
    n, a, b = blocks.shape
    eye = jnp.eye(n, dtype=blocks.dtype)
    return (eye[:, None, :, None] * blocks[:, :, None, :]).reshape(n * a, n * b)


def _pad_cols(w, n):
    return jnp.pad(w, [(0, 0)] * (w.ndim - 1) + [(0, n - w.shape[-1])])


def _pad_rows(w, n):
    return jnp.pad(w, [(0, 0)] * (w.ndim - 2) + [(0, n - w.shape[-2]), (0, 0)])


def _s5_perms():
    perm = np.zeros((2, TILE, TILE), np.float32)
    for j in range(S5_STEPS):
        for s in range(S5_SUB):
            t = s * S5_STEPS + j
            perm[0, j * S5_SUB + s, t] = 1.0
            perm[1, j * S5_SUB + s, TILE - 1 - t] = 1.0
    return jnp.asarray(perm, BF16), jnp.asarray(np.transpose(perm, (0, 2, 1)), BF16)


def _ones_blocks(width):
    idx = np.arange(BRANCH_W) // width
    return jnp.asarray(idx[:, None] == idx[None, :], BF16)


def _layer_params(q, i):
    p = {}
    row = lambda v: v[i].reshape(1, -1)
    dt = jnp.exp(q['s5_log_step'][i])[:, :, None]
    lam_re, lam_im = q['s5_lam_re'][i], q['s5_lam_im'][i]
    mag = jnp.exp(lam_re * dt)
    a_re, a_im = mag * jnp.cos(lam_im * dt), mag * jnp.sin(lam_im * dt)
    den = lam_re * lam_re + lam_im * lam_im
    f_re = ((a_re - 1.0) * lam_re + a_im * lam_im) / den
    f_im = (a_im * lam_re - (a_re - 1.0) * lam_im) / den
    b_re, b_im = q['s5_b_re'][i], q['s5_b_im'][i]
    d_re = f_re[..., None] * b_re[None] - f_im[..., None] * b_im[None]
    d_im = f_re[..., None] * b_im[None] + f_im[..., None] * b_re[None]
    gh = S5_G // 2
    halves = [slice(0, gh), slice(gh, S5_G)]
    bd = lambda blocks: _block_diag(jnp.swapaxes(blocks, 1, 2))
    p['s5_drive'] = jnp.stack([jnp.stack([jnp.concatenate([bd(d_re[dr, hs]), bd(d_im[dr, hs])], axis=1)
                                          for hs in halves]) for dr in range(2)]).astype(BF16)
    p['s5_a'] = jnp.concatenate([jnp.concatenate([a_re[:, hs].reshape(2, 1, -1), a_im[:, hs].reshape(2, 1, -1)],
                                                 axis=-1) for hs in halves], axis=-1)
    c_re, c_im = q['s5_c_re'][i], q['s5_c_im'][i]
    p['s5_read'] = jnp.stack([jnp.concatenate([bd(c_re[hs]), -bd(c_im[hs])], axis=0)
                              for hs in halves]).astype(BF16)
    p['s5_perm'], p['s5_perm_t'] = _s5_perms()
    p['s5_d'] = row(q['s5_d'])
    p['s5_w_glu'] = q['s5_w_glu'][i].astype(BF16)
    p['gla_w1'] = _pad_cols(q['gla_w1'][i], LORA_PAD)
    p['gla_w2'] = _pad_rows(q['gla_w2'][i], LORA_PAD)
    p['gla_bias'] = q['gla_bias'][i]
    p['gla_norm_w'] = jnp.tile(q['gla_norm_w'][i], GLA_HEADS).reshape(1, -1)
    p['lru_conv_w'] = q['lru_conv_w'][i]
    p['lru_conv_b'] = row(q['lru_conv_b'])
    p['lru_wa'] = jnp.stack([_block_diag(q['lru_w_a'][i, dr]) for dr in range(2)]).astype(BF16)
    p['lru_wx'] = jnp.stack([_block_diag(q['lru_w_x'][i, dr]) for dr in range(2)]).astype(BF16)
    p['lru_b_a'], p['lru_b_x'] = q['lru_b_a'][i], q['lru_b_x'][i]
    p['lru_c'] = -LRU_C * jax.nn.softplus(-q['lru_lam'][i])
    p['rw_mu'] = q['rw_mu'][i]
    for name in ('rw_k_k', 'rw_k_a', 'rw_r_k', 'rw_ln_w', 'rw_ln_b'):
        p[name] = row(q[name])
    p['rw_w0'], p['rw_a0'] = q['rw_w0'][i], q['rw_a0'][i]
    p['rw_w1'] = _pad_cols(q['rw_w1'][i], LORA_PAD)
    p['rw_w2'] = _pad_rows(q['rw_w2'][i], LORA_PAD)
    p['rw_a1'] = _pad_cols(q['rw_a1'][i], LORA_PAD)
    p['rw_a2'] = _pad_rows(q['rw_a2'][i], LORA_PAD)
    p['rw_g1'] = q['rw_g1'][i].astype(BF16)
    p['rw_g2'] = q['rw_g2'][i].astype(BF16)
    p['ones64'] = _ones_blocks(RW_N)
    p['ones128'] = _ones_blocks(GLA_DV)
    p['w_in'] = q['w_in'][i].astype(BF16)
    p['w_gate'] = q['w_gate'][i].astype(BF16)
    p['b_gate'] = row(q['b_gate'])
    p['w_branch'] = q['w_branch'][i].astype(BF16)
    p['w_out'] = q['w_out'][i].astype(BF16)
    p['ffn_w_up'] = q['ffn_w_up'][i].astype(BF16)
    p['ffn_w_down'] = q['ffn_w_down'][i].astype(BF16)
    p['ffn_conv_w'] = q['ffn_conv_w'][i]
    p['ffn_conv_b'] = row(q['ffn_conv_b'])
    for name in ('ln1_g', 'ln1_b', 'ln2_g', 'ln2_b'):
        p[name] = row(q[name])
    return p


def kernel(x, c, ctx, c_ctx, w_mod, b_mod, w_in, s5_lam_re, s5_lam_im, s5_log_step, s5_b_re, s5_b_im, s5_c_re, s5_c_im, s5_d, s5_w_glu, gla_w1, gla_w2, gla_bias, gla_norm_w, lru_conv_w, lru_conv_b, lru_w_a, lru_b_a, lru_w_x, lru_b_x, lru_lam, rw_mu, rw_w0, rw_w1, rw_w2, rw_a0, rw_a1, rw_a2, rw_g1, rw_g2, rw_k_k, rw_k_a, rw_r_k, rw_ln_w, rw_ln_b, w_branch, w_gate, b_gate, w_out, ln1_g, ln1_b, ln2_g, ln2_b, ffn_w_up, ffn_conv_w, ffn_conv_b, ffn_w_down):
    q = dict(w_in=w_in, s5_lam_re=s5_lam_re, s5_lam_im=s5_lam_im, s5_log_step=s5_log_step, s5_b_re=s5_b_re,
             s5_b_im=s5_b_im, s5_c_re=s5_c_re, s5_c_im=s5_c_im, s5_d=s5_d, s5_w_glu=s5_w_glu,
             gla_w1=gla_w1, gla_w2=gla_w2, gla_bias=gla_bias, gla_norm_w=gla_norm_w,
             lru_conv_w=lru_conv_w, lru_conv_b=lru_conv_b, lru_w_a=lru_w_a, lru_b_a=lru_b_a,
             lru_w_x=lru_w_x, lru_b_x=lru_b_x, lru_lam=lru_lam,
             rw_mu=rw_mu, rw_w0=rw_w0, rw_w1=rw_w1, rw_w2=rw_w2, rw_a0=rw_a0, rw_a1=rw_a1, rw_a2=rw_a2,
             rw_g1=rw_g1, rw_g2=rw_g2, rw_k_k=rw_k_k, rw_k_a=rw_k_a, rw_r_k=rw_r_k,
             rw_ln_w=rw_ln_w, rw_ln_b=rw_ln_b,
             w_branch=w_branch, w_gate=w_gate, b_gate=b_gate, w_out=w_out,
             ln1_g=ln1_g, ln1_b=ln1_b, ln2_g=ln2_g, ln2_b=ln2_b,
             ffn_w_up=ffn_w_up, ffn_conv_w=ffn_conv_w, ffn_conv_b=ffn_conv_b, ffn_w_down=ffn_w_down)
    batch, seq, d = x.shape
    assert batch == 1 and d == D_MODEL and ctx.shape[1] == CTX_LEN and seq % TILE == 0
    depth = w_mod.shape[0]
    dn_alpha = (2 * depth) ** 0.25
    n_tiles = 1 + seq // TILE

    cc = jnp.zeros((8, D_MODEL), F32).at[0].set(c[0]).at[1].set(c_ctx)
    mods = _modulation(cc, w_mod, b_mod)
    xs = _ln0(jnp.concatenate([ctx[0], x[0]], axis=0))
    for i in range(depth):
        p = _layer_params(q, i)
        mod = mods[i]
        proj = _inproj(xs, mod, p)
        s5f, s5b = _s5(proj, n_tiles, p)
        glf, glb = _gla(proj, n_tiles, p)
        lrf, lrb = _lru(proj, n_tiles, p)
        rwf, rwb, bonus, rgate = _rwkv(proj, n_tiles, p)
        xs = _merge(xs, mod, proj, (s5f, s5b, glf, glb, lrf, lrb, rwf, rwb, bonus, rgate), n_tiles, p, dn_alpha)
        xs = _ffn(xs, mod, p, dn_alpha)
    return xs[TILE:].reshape(batch, seq, d)
```

```python
import functools
import math

import numpy as np
import jax
import jax.numpy as jnp
from jax import lax
from jax.experimental import pallas as pl
from jax.experimental.pallas import tpu as pltpu

F32 = jnp.float32
BF16 = jnp.bfloat16

D_MODEL = 1024
GRID_W = 64
CTX_LEN = 256
TILE = CTX_LEN
BRANCH_W = 512
N_BRANCH = 4

S5_H, S5_P = 16, 64
S5_G = BRANCH_W // S5_H
S5_STATE = S5_G * S5_P
S5_SUB = 8
S5_STEPS = TILE // S5_SUB
S5_UNROLL = True

GLA_HEADS, GLA_DK = 4, 64
GLA_DV = BRANCH_W // GLA_HEADS
GLA_QK = GLA_HEADS * GLA_DK
GLA_TAU = 16.0
CHUNK = 64
N_CHUNK = TILE // CHUNK

LRU_C = 8.0
LRU_CONV_LEFT = 2

RW_N = 64
RW_GROUP = 256
RW_GN_EPS = 64e-5
RW_DECAY_SCALE = math.exp(-0.5)

FFN_HIDDEN = 2816
FFN_CHUNKS = (1024, 1024, 768)
assert sum(FFN_CHUNKS) == FFN_HIDDEN
LORA_PAD = 128

LN_EPS = 1e-6
SUBLANES = 8
V7X_VMEM_LIMIT_BYTES = 56 * 1024 * 1024


def _bf(x):
    return x.astype(BF16)


def _mm(a, b):
    return jnp.dot(_bf(a), _bf(b), preferred_element_type=F32)


def _dot(a, b):
    return jnp.dot(a, b, preferred_element_type=F32)


def _mm_nt(a, b):
    return lax.dot_general(_bf(a), _bf(b), (((1,), (1,)), ((), ())), preferred_element_type=F32)


def _mm_tn(a, b):
    return lax.dot_general(_bf(a), _bf(b), (((0,), (0,)), ((), ())), preferred_element_type=F32)


def _split2(x):
    hi = _bf(x)
    lo = _bf(x - hi.astype(F32))
    return hi, lo


def _split3(x):
    hi = _bf(x)
    r = x - hi.astype(F32)
    mid = _bf(r)
    lo = _bf(r - mid.astype(F32))
    return hi, mid, lo


def _mm_hp(a, b):
    ah, al = _split2(a)
    bh, bl = _split2(b)
    dot = functools.partial(jnp.dot, preferred_element_type=F32)
    return dot(ah, bh) + (dot(ah, bl) + dot(al, bh))


def _mm_exact_rhs(a, b_exact):
    ah, al = _split2(a)
    dot = functools.partial(jnp.dot, preferred_element_type=F32)
    return dot(ah, b_exact) + dot(al, b_exact)


def _seg_sum(x, ones_blk):
    return _mm_exact_rhs(x, ones_blk)


def _iota(shape, dim):
    return lax.broadcasted_iota(jnp.int32, shape, dim)


def _sigmoid(x):
    return 0.5 * jnp.tanh(0.5 * x) + 0.5


def _silu(x):
    return x * _sigmoid(x)


def _gelu(x):
    return 0.5 * x * (1.0 + jnp.tanh(math.sqrt(2.0 / math.pi) * (x + 0.044715 * (x * x * x))))


def _log_sigmoid(x):
    return jnp.minimum(x, 0.0) - jnp.log1p(jnp.exp(-jnp.abs(x)))


def _neg_expm1_twice(y):
    t = jnp.tanh(y)
    return (-2.0 * t) / (1.0 - t)


def _layer_norm(x, eps):
    mu = jnp.mean(x, axis=-1, keepdims=True)
    xc = x - mu
    var = jnp.mean(xc * xc, axis=-1, keepdims=True)
    return xc * lax.rsqrt(var + eps)


def _shift_masks(shape, shifts, is_ctx):
    n = shape[0]
    t = _iota(shape, 0)
    pos = jnp.where(is_ctx, t, t & (GRID_W - 1))
    row_len = jnp.where(is_ctx, n, GRID_W)
    return {k: ((pos + k >= 0) & (pos + k < row_len)).astype(F32) for k in shifts}


def _shift_rows(x, k, masks):
    return pltpu.roll(x, (-k) % x.shape[0], axis=0) * masks[k]


def _seg_cumsum(x, reverse):
    n = x.shape[0]
    t = _iota(x.shape, 0) & (CHUNK - 1)
    d = 1
    while d < CHUNK:
        if reverse:
            x = x + jnp.where(t < CHUNK - d, pltpu.roll(x, n - d, axis=0), 0.0)
        else:
            x = x + jnp.where(t >= d, pltpu.roll(x, d, axis=0), 0.0)
        d *= 2
    return x


def _scan_groups(a, b, reverse):
    n = a.shape[0]
    t = _iota(a.shape, 0) & (SUBLANES - 1)
    d = 1
    while d < SUBLANES:
        if reverse:
            valid, sh = t < SUBLANES - d, n - d
        else:
            valid, sh = t >= d, d
        a_s = jnp.where(valid, pltpu.roll(a, sh, axis=0), 1.0)
        b_s = jnp.where(valid, pltpu.roll(b, sh, axis=0), 0.0)
        b = b + a * b_s
        a = a * a_s
        d *= 2
    return a, b


def _chain_groups(scans):
    n_grp = scans[0][0].shape[0] // SUBLANES
    parts = [[None] * n_grp for _ in scans]
    carries = [s[2] for s in scans]
    for step in range(n_grp):
        for i, (a, b, _, reverse) in enumerate(scans):
            gi = n_grp - 1 - step if reverse else step
            rows = slice(gi * SUBLANES, (gi + 1) * SUBLANES)
            h = b[rows] + a[rows] * carries[i]
            parts[i][gi] = h
            last = 0 if reverse else SUBLANES - 1
            carries[i] = h[last:last + 1]
    return [(jnp.concatenate(p, axis=0), c) for p, c in zip(parts, carries)]


def _chunk_rows(x, row):
    parts = [jnp.broadcast_to(x[c * CHUNK + row:c * CHUNK + row + 1], (CHUNK, x.shape[1]))
             for c in range(x.shape[0] // CHUNK)]
    return jnp.concatenate(parts, axis=0)


def _tile4(x):
    return jnp.concatenate([x] * (RW_GROUP // CHUNK), axis=0)


def _block_mask(rows, cols, row_blk, col_blk):
    return (_iota((rows, cols), 0) // row_blk) == (_iota((rows, cols), 1) // col_blk)


def _causal_cat(reverse, strict):
    t = _iota((CHUNK, RW_GROUP), 0)
    s = _iota((CHUNK, RW_GROUP), 1) & (CHUNK - 1)
    if reverse:
        return (s > t) if strict else (s >= t)
    return (s < t) if strict else (s <= t)


def _bwd_tile(i, n_tiles):
    return jnp.where(i == 0, 0, n_tiles - i)


def _cparams():
    return pltpu.CompilerParams(dimension_semantics=("arbitrary",), vmem_limit_bytes=V7X_VMEM_LIMIT_BYTES)


def _full(arr):
    nd = arr.ndim
    return pl.BlockSpec(arr.shape, lambda i, _nd=nd: (0,) * _nd)


def _ln0_kernel(ctx_ref, x_ref, o_ref):
    is_ctx = pl.program_id(0) == 0
    o_ref[...] = _layer_norm(jnp.where(is_ctx, ctx_ref[...], x_ref[...]), LN_EPS)


def _ln0(ctx2d, x2d):
    n_tok = ctx2d.shape[0] + x2d.shape[0]
    return pl.pallas_call(
        _ln0_kernel,
        grid=(n_tok // TILE,),
        in_specs=[pl.BlockSpec((TILE, D_MODEL), lambda i: (0, 0)),
                  pl.BlockSpec((TILE, D_MODEL), lambda i: (jnp.maximum(i - 1, 0), 0))],
        out_specs=pl.BlockSpec((TILE, D_MODEL), lambda i: (i, 0)),
        out_shape=jax.ShapeDtypeStruct((n_tok, D_MODEL), F32),
        compiler_params=_cparams(),
        name="ln0",
    )(ctx2d, x2d)


MOD_COLS = 1536


def _mod_kernel(c_ref, w_ref, b_ref, o_ref):
    s = _silu(c_ref[...])
    o_ref[0] = _mm_hp(s, w_ref[0]) + b_ref[0]


def _modulation(cc, w_mod, b_mod):
    depth = w_mod.shape[0]
    n_col = w_mod.shape[2] // MOD_COLS
    return pl.pallas_call(
        _mod_kernel,
        grid=(depth, n_col),
        in_specs=[pl.BlockSpec((8, D_MODEL), lambda l, j: (0, 0)),
                  pl.BlockSpec((1, D_MODEL, MOD_COLS), lambda l, j: (l, 0, j)),
                  pl.BlockSpec((1, 1, MOD_COLS), lambda l, j: (l, 0, j))],
        out_specs=pl.BlockSpec((1, 8, MOD_COLS), lambda l, j: (l, 0, j)),
        out_shape=jax.ShapeDtypeStruct((depth, 8, w_mod.shape[2]), F32),
        compiler_params=pltpu.CompilerParams(dimension_semantics=("arbitrary", "arbitrary"),
                                             vmem_limit_bytes=V7X_VMEM_LIMIT_BYTES),
        name="modulation",
    )(cc, w_mod, b_mod.reshape(depth, 1, -1))


def _mod_rows(mod_ref, is_ctx):
    m = mod_ref[...]
    return jnp.where(is_ctx, m[1:2], m[0:1])


def _mod_part(m, j):
    return m[:, j * D_MODEL:(j + 1) * D_MODEL]


(COL_S5_U, COL_GLA_QK, COL_GLA_V, COL_GLA_G, COL_LRU_X, COL_LRU_GATE, COL_RW_R, COL_RW_K, COL_RW_V,
 COL_RW_KK) = range(10)
N_PROJ_COLS = 10 * BRANCH_W


def _col(block):
    return slice(block * BRANCH_W, (block + 1) * BRANCH_W)


def _inproj_kernel(x_ref, mod_ref, w_ref, cw_ref, cb_ref, mu_ref, kkw_ref, ones_ref, o_ref):
    is_ctx = pl.program_id(0) == 0
    m = _mod_rows(mod_ref, is_ctx)
    h = _bf(x_ref[...] * (1.0 + _mod_part(m, 1)) + _mod_part(m, 0))
    n_taps = cw_ref.shape[0]
    taps = [k - LRU_CONV_LEFT for k in range(n_taps)]
    masks = _shift_masks((TILE, BRANCH_W), sorted(set(taps + [-1, 1]) - {0}), is_ctx)

    x = _dot(h, w_ref[:, _col(COL_LRU_X)])
    o_ref[:, _col(COL_LRU_X)] = cb_ref[...] + sum(
        cw_ref[j:j + 1] * (_shift_rows(x, k, masks) if k else x) for j, k in enumerate(taps))

    for j, block in enumerate((COL_RW_R, COL_RW_K, COL_RW_V)):
        x = _dot(h, w_ref[:, _col(block)])
        xs = 0.5 * (_shift_rows(x, -1, masks) + _shift_rows(x, 1, masks))
        x = x + mu_ref[j:j + 1] * (xs - x)
        o_ref[:, _col(block)] = x
        if block == COL_RW_K:
            kk = x * kkw_ref[...]
            o_ref[:, _col(COL_RW_KK)] = kk / jnp.maximum(jnp.sqrt(_seg_sum(kk * kk, ones_ref[...])), 1e-12)

    plain = slice(0, COL_LRU_X * BRANCH_W)
    o_ref[:, plain] = _dot(h, w_ref[:, plain])
    o_ref[:, _col(COL_LRU_GATE)] = _dot(h, w_ref[:, _col(COL_LRU_GATE)])


def _inproj(xs, mod, p):
    n_tok = xs.shape[0]
    consts = (p['w_in'], p['lru_conv_w'], p['lru_conv_b'], p['rw_mu'], p['rw_k_k'], p['ones64'])
    return pl.pallas_call(
        _inproj_kernel,
        grid=(n_tok // TILE,),
        in_specs=[pl.BlockSpec((TILE, D_MODEL), lambda i: (i, 0)), _full(mod)] + [_full(c) for c in consts],
        out_specs=pl.BlockSpec((TILE, N_PROJ_COLS), lambda i: (i, 0)),
        out_shape=jax.ShapeDtypeStruct((n_tok, N_PROJ_COLS), F32),
        compiler_params=_cparams(),
        name="inproj",
    )(xs, mod, *consts)


def _proj_specs(n_tiles, col):
    fwd = pl.BlockSpec((TILE, BRANCH_W), lambda i: (i, col))
    bwd = pl.BlockSpec((TILE, BRANCH_W), lambda i: (_bwd_tile(i, n_tiles), col))
    return fwd, bwd


def _out_specs(n_tiles):
    fwd = pl.BlockSpec((TILE, BRANCH_W), lambda i: (i, 0))
    bwd = pl.BlockSpec((TILE, BRANCH_W), lambda i: (_bwd_tile(i, n_tiles), 0))
    return fwd, bwd


def _lru_kernel(xf_ref, xb_ref, wa_ref, ba_ref, wx_ref, bx_ref, lc_ref,
                hf_ref, hb_ref, carry_ref):
    @pl.when(pl.program_id(0) == 0)
    def _():
        carry_ref[...] = jnp.zeros_like(carry_ref)

    scans = []
    for dr, x_ref in enumerate((xf_ref, xb_ref)):
        xc = x_ref[...]
        r = _sigmoid(_mm(xc, wa_ref[dr]) + ba_ref[dr:dr + 1])
        ig = _sigmoid(_mm(xc, wx_ref[dr]) + bx_ref[dr:dr + 1])
        log_a = lc_ref[dr:dr + 1] * r
        a = jnp.exp(log_a)
        b = jnp.sqrt(_neg_expm1_twice(log_a)) * (ig * xc)
        scans.append(_scan_groups(a, b, reverse=dr == 1) + (carry_ref[dr:dr + 1], dr == 1))
    for dr, ((h, carry), o_ref) in enumerate(zip(_chain_groups(scans), (hf_ref, hb_ref))):
        o_ref[...] = h
        carry_ref[dr:dr + 1] = carry


def _lru(proj, n_tiles, p):
    xf, xb = _proj_specs(n_tiles, COL_LRU_X)
    of, ob = _out_specs(n_tiles)
    n_tok = proj.shape[0]
    consts = (p['lru_wa'], p['lru_b_a'], p['lru_wx'], p['lru_b_x'], p['lru_c'])
    out = jax.ShapeDtypeStruct((n_tok, BRANCH_W), F32)
    return pl.pallas_call(
        _lru_kernel,
        grid=(n_tiles,),
        in_specs=[xf, xb] + [_full(c) for c in consts],
        out_specs=[of, ob],
        out_shape=[out, out],
        scratch_shapes=[pltpu.VMEM((8, BRANCH_W), F32)],
        compiler_params=_cparams(),
        name="lru_scan",
    )(proj, proj, *consts)


S5_HALF = S5_STATE // 2


def _cmul(ar, ai, br, bi):
    return ar * br - ai * bi, ar * bi + ai * br


def _s5_cols(half):
    base = 2 * S5_HALF * half
    return slice(base, base + S5_HALF), slice(base + S5_HALF, base + 2 * S5_HALF)


def _s5_kernel(uf_ref, ub_ref, perm_ref, permt_ref, a_ref, drive_ref, read_ref,
               yf_ref, yb_ref, h_ref, carry_ref, pw_ref):
    @pl.when(pl.program_id(0) == 0)
    def _():
        carry_ref[...] = jnp.zeros_like(carry_ref)
        for dr in range(2):
            for half in range(2):
                re, im = _s5_cols(half)
                a_re, a_im = a_ref[dr, :, re], a_ref[dr, :, im]
                q_re, q_im = a_re, a_im
                for j in range(S5_STEPS):
                    rows = slice(j * S5_SUB, (j + 1) * S5_SUB)
                    pw_ref[dr, rows, re] = jnp.broadcast_to(q_re, (S5_SUB, S5_HALF))
                    pw_ref[dr, rows, im] = jnp.broadcast_to(q_im, (S5_SUB, S5_HALF))
                    q_re, q_im = _cmul(q_re, q_im, a_re, a_im)

    sub = _iota((S5_SUB, S5_HALF), 0)
    half_u = BRANCH_W // 2
    streams = ((uf_ref, yf_ref), (ub_ref, yb_ref))
    for dr, (u_ref, _) in enumerate(streams):
        u_p = _bf(jnp.dot(perm_ref[dr], _bf(u_ref[...]), preferred_element_type=F32))
        for half in range(2):
            h_ref[dr, :, 2 * S5_HALF * half:2 * S5_HALF * (half + 1)] = _dot(
                u_p[:, half * half_u:(half + 1) * half_u], drive_ref[dr, half])
    for dr, (_, o_ref) in enumerate(streams):
        for half in range(2):
            re, im = _s5_cols(half)
            a_re = a_ref[dr, :, re]
            a_im = a_ref[dr, :, im]
            a_re8 = jnp.broadcast_to(a_re, (S5_SUB, S5_HALF))
            a_im8 = jnp.broadcast_to(a_im, (S5_SUB, S5_HALF))

            def local_step(j, carry, dr=dr, re=re, im=im, a_re8=a_re8, a_im8=a_im8):
                h_re, h_im = carry
                rows = pl.ds(pl.multiple_of(j * S5_SUB, S5_SUB), S5_SUB)
                m_re, m_im = _cmul(a_re8, a_im8, h_re, h_im)
                h_re = m_re + h_ref[dr, rows, re]
                h_im = m_im + h_ref[dr, rows, im]
                h_ref[dr, rows, re] = h_re
                h_ref[dr, rows, im] = h_im
                return h_re, h_im

            zero = jnp.zeros((S5_SUB, S5_HALF), F32)
            end_re, end_im = lax.fori_loop(0, S5_STEPS, local_step, (zero, zero), unroll=S5_UNROLL)

            p_re, p_im = a_re, a_im
            for _ in range(int(math.log2(S5_STEPS))):
                p_re, p_im = _cmul(p_re, p_im, p_re, p_im)
            s_re = carry_ref[dr, :, re]
            s_im = carry_ref[dr, :, im]
            in_re = jnp.zeros((S5_SUB, S5_HALF), F32)
            in_im = jnp.zeros((S5_SUB, S5_HALF), F32)
            for s in range(S5_SUB):
                in_re = jnp.where(sub == s, s_re, in_re)
                in_im = jnp.where(sub == s, s_im, in_im)
                m_re, m_im = _cmul(p_re, p_im, s_re, s_im)
                s_re = m_re + end_re[s:s + 1]
                s_im = m_im + end_im[s:s + 1]
            carry_ref[dr, :, re] = s_re
            carry_ref[dr, :, im] = s_im

            def fix_step(j, carry, dr=dr, re=re, im=im, in_re=in_re, in_im=in_im):
                rows = pl.ds(pl.multiple_of(j * S5_SUB, S5_SUB), S5_SUB)
                c_re, c_im = _cmul(pw_ref[dr, rows, re], pw_ref[dr, rows, im], in_re, in_im)
                h_ref[dr, rows, re] = h_ref[dr, rows, re] + c_re
                h_ref[dr, rows, im] = h_ref[dr, rows, im] + c_im
                return carry

            lax.fori_loop(0, S5_STEPS, fix_step, 0, unroll=S5_UNROLL)

        y_p = jnp.concatenate(
            [_mm(h_ref[dr, :, 2 * S5_HALF * half:2 * S5_HALF * (half + 1)], read_ref[half]) for half in range(2)],
            axis=1)
        hi, lo = _split2(y_p)
        o_ref[...] = _dot(permt_ref[dr], hi) + _dot(permt_ref[dr], lo)


def _s5(proj, n_tiles, p):
    uf, ub = _proj_specs(n_tiles, COL_S5_U)
    of, ob = _out_specs(n_tiles)
    n_tok = proj.shape[0]
    consts = (p['s5_perm'], p['s5_perm_t'], p['s5_a'], p['s5_drive'], p['s5_read'])
    out = jax.ShapeDtypeStruct((n_tok, BRANCH_W), F32)
    return pl.pallas_call(
        _s5_kernel,
        grid=(n_tiles,),
        in_specs=[uf, ub] + [_full(c) for c in consts],
        out_specs=[of, ob],
        out_shape=[out, out],
        scratch_shapes=[pltpu.VMEM((2, TILE, 2 * S5_STATE), F32), pltpu.VMEM((2, 1, 2 * S5_STATE), F32),
                        pltpu.VMEM((2, TILE, 2 * S5_STATE), F32)],
        compiler_params=_cparams(),
        name="s5_scan",
    )(proj, proj, *consts)


def _gla_kernel(qkf_ref, vf_ref, qkb_ref, vb_ref, w1_ref, w2_ref, bias_ref, of_ref, ob_ref, st_ref):
    @pl.when(pl.program_id(0) == 0)
    def _():
        st_ref[...] = jnp.zeros_like(st_ref)

    bm_k = _bf(_block_mask(RW_GROUP, GLA_QK, CHUNK, GLA_DK).astype(F32))
    bm_v = _bf(_block_mask(RW_GROUP, BRANCH_W, CHUNK, GLA_DV).astype(F32))
    bm_s = _block_mask(BRANCH_W, GLA_QK, GLA_DV, GLA_DK)
    streams = ((qkf_ref, vf_ref, of_ref), (qkb_ref, vb_ref, ob_ref))
    prep = []
    for dr, (qk_ref, v_ref, _) in enumerate(streams):
        reverse = dr == 1
        q = qk_ref[:, :GLA_QK] * (GLA_DK ** -0.5)
        k = qk_ref[:, GLA_QK:]
        la = _log_sigmoid(_mm(_mm(k, w1_ref[dr]), w2_ref[dr]) + bias_ref[dr:dr + 1]) * (1.0 / GLA_TAU)
        b = _seg_cumsum(la, reverse)
        b_end = _chunk_rows(b, 0 if reverse else CHUNK - 1)
        b_mid = _chunk_rows(b, CHUNK - 1 - CHUNK // 2 if reverse else CHUNK // 2)
        prep.append(dict(qe=_bf(q * jnp.exp(b - b_mid)), ke=_bf(k * jnp.exp(b_mid - b)),
                         kl=_bf(k * jnp.exp(b_end - b)), qb=_bf(q * jnp.exp(b)), decay=jnp.exp(b_end),
                         v=_bf(v_ref[...]), causal=_causal_cat(reverse, strict=False)))
    states = [st_ref[0], st_ref[1]]
    for step in range(N_CHUNK):
        for dr, (_, _, o_ref) in enumerate(streams):
            c = N_CHUNK - 1 - step if dr == 1 else step
            rows = slice(c * CHUNK, (c + 1) * CHUNK)
            t = prep[dr]
            ke_blk = _tile4(t['ke'][rows]) * bm_k
            att = jnp.where(t['causal'], _mm_nt(t['qe'][rows], ke_blk), 0.0)
            v_blk = _tile4(t['v'][rows]) * bm_v
            o_ref[rows, :] = _mm(att, v_blk) + _mm_nt(t['qb'][rows], states[dr])
            states[dr] = (states[dr] * t['decay'][c * CHUNK:c * CHUNK + 1]
                          + jnp.where(bm_s, _mm_tn(t['v'][rows], t['kl'][rows]), 0.0))
    st_ref[0], st_ref[1] = states


def _gla(proj, n_tiles, p):
    qkf, qkb = _proj_specs(n_tiles, COL_GLA_QK)
    vf, vb = _proj_specs(n_tiles, COL_GLA_V)
    of, ob = _out_specs(n_tiles)
    n_tok = proj.shape[0]
    consts = (p['gla_w1'], p['gla_w2'], p['gla_bias'])
    out = jax.ShapeDtypeStruct((n_tok, BRANCH_W), F32)
    return pl.pallas_call(
        _gla_kernel,
        grid=(n_tiles,),
        in_specs=[qkf, vf, qkb, vb] + [_full(c) for c in consts],
        out_specs=[of, ob],
        out_shape=[out, out],
        scratch_shapes=[pltpu.VMEM((2, BRANCH_W, GLA_QK), F32)],
        compiler_params=_cparams(),
        name="gla_scan",
    )(proj, proj, proj, proj, *consts)


RW_GROUPS = BRANCH_W // RW_GROUP
RW_ITEMS = 2 * RW_GROUPS * N_CHUNK
OP_A_T, OP_R_T, OP_B_T, OP_K_T, OP_A_TRUE, OP_B_HAT, OP_K_HAT, OP_V = range(8)


def _rwkv_kernel(rf_ref, kf_ref, vf_ref, kkf_ref, rb_ref, kb_ref, vb_ref, kkb_ref,
                 ka_ref, rk_ref, ones_ref,
                 w0_ref, wa1_ref, w2_ref, a0_ref, a2_ref, g1_ref, g2_ref,
                 yf_ref, yb_ref, bonus_ref, gate_ref,
                 zt_ref, ops_ref, rtrue_ref, gam_ref, l_ref, x_ref, gak_ref, grb_ref, grk_ref,
                 rw_ref, yloc_ref, uloc_ref):
    is_ctx = pl.program_id(0) == 0

    @pl.when(is_ctx)
    def _():
        zt_ref[...] = jnp.zeros_like(zt_ref)

    streams = ((rf_ref, kf_ref, vf_ref, yf_ref), (rb_ref, kb_ref, vb_ref, yb_ref))
    for dr, (r_ref, k_ref, v_ref, _) in enumerate(streams):
        reverse = dr == 1
        r, k, v = r_ref[...], k_ref[...], v_ref[...]
        kk = (kkb_ref if reverse else kkf_ref)[...]
        if dr == 0:
            bonus_ref[...] = _seg_sum(r * k * rk_ref[...], ones_ref[...]) * v
            gate_ref[...] = _mm(_sigmoid(_mm(r, g1_ref[...])), g2_ref[...])
        low = _mm(k, wa1_ref[dr])
        lw = -RW_DECAY_SCALE * _sigmoid(w0_ref[dr:dr + 1] + _mm(jnp.tanh(low[:, :LORA_PAD]), w2_ref[dr]))
        ag = _sigmoid(a0_ref[dr:dr + 1] + _mm(low[:, LORA_PAD:], a2_ref[dr]))
        kd = k * (1.0 + (ag - 1.0) * ka_ref[...])
        a = -kk
        b = kk * ag
        g = _seg_cumsum(lw, reverse)
        g_prev = g - lw
        g_end = _chunk_rows(g, 0 if reverse else CHUNK - 1)
        g_mid = _chunk_rows(g, CHUNK - 1 - CHUNK // 2 if reverse else CHUNK // 2)
        e_mid = jnp.exp(g_mid - g)
        e_end = jnp.exp(g_end - g)
        ops_ref[dr, OP_A_T] = _bf(a * jnp.exp(g_prev - g_mid))
        ops_ref[dr, OP_R_T] = _bf(r * jnp.exp(g - g_mid))
        ops_ref[dr, OP_B_T] = _bf(b * e_mid)
        ops_ref[dr, OP_K_T] = _bf(kd * e_mid)
        ops_ref[dr, OP_A_TRUE] = _bf(a * jnp.exp(g_prev))
        ops_ref[dr, OP_B_HAT] = _bf(b * e_end)
        ops_ref[dr, OP_K_HAT] = _bf(kd * e_end)
        ops_ref[dr, OP_V] = _bf(v)
        rtrue_ref[dr] = r * jnp.exp(g)
        gam_ref[dr] = jnp.exp(g_end)

    bm = ones_ref[0:RW_GROUP, 0:RW_GROUP]
    eye_cat = (_iota((CHUNK, RW_GROUP), 0) == (_iota((CHUNK, RW_GROUP), 1) & (CHUNK - 1))).astype(F32)

    def blk(x):
        return _tile4(x) * bm

    items = [(dr, grp, c) for dr in range(2) for grp in range(RW_GROUPS) for c in range(N_CHUNK)]

    def sl(c, grp):
        return slice(c * CHUNK, (c + 1) * CHUNK), slice(grp * RW_GROUP, (grp + 1) * RW_GROUP)

    for i, (dr, grp, c) in enumerate(items):
        rows, cols = sl(c, grp)
        lhs = jnp.concatenate([ops_ref[dr, OP_A_T, rows, cols], ops_ref[dr, OP_R_T, rows, cols]], axis=0)
        rhs = jnp.concatenate([blk(ops_ref[dr, OP_B_T, rows, cols]), blk(ops_ref[dr, OP_K_T, rows, cols])], axis=0)
        g = lax.dot_general(lhs, rhs, (((1,), (1,)), ((), ())), preferred_element_type=F32)
        strict = _causal_cat(dr == 1, strict=True)
        incl = _causal_cat(dr == 1, strict=False)
        g_ab = jnp.where(strict, g[:CHUNK, :RW_GROUP], 0.0)
        l_ref[i] = g_ab
        x_ref[i] = eye_cat + g_ab
        gak_ref[i] = _bf(jnp.where(strict, g[:CHUNK, RW_GROUP:], 0.0))
        grb_ref[i] = _bf(jnp.where(incl, g[CHUNK:, :RW_GROUP], 0.0))
        grk_ref[i] = _bf(jnp.where(incl, g[CHUNK:, RW_GROUP:], 0.0))

    n_sq = int(math.log2(CHUNK)) - 1
    for i in range(RW_ITEMS):
        lp = _bf(l_ref[i])
        l_ref[i] = _dot(lp, blk(lp))
    for step in range(1, n_sq):
        for i in range(RW_ITEMS):
            lp = _bf(l_ref[i])
            x = x_ref[i]
            res = _dot(jnp.concatenate([_bf(x), lp], axis=0), blk(lp))
            x_ref[i] = x + res[:CHUNK]
            l_ref[i] = res[CHUNK:]
    for i in range(RW_ITEMS):
        x = x_ref[i]
        x_ref[i] = x + _dot(_bf(x), blk(_bf(l_ref[i])))

    for i, (dr, grp, c) in enumerate(items):
        rows, cols = sl(c, grp)
        gak_ref[i] = _bf(_dot(gak_ref[i], blk(ops_ref[dr, OP_V, rows, cols])))

    for i, (dr, grp, c) in enumerate(items):
        rows, cols = sl(c, grp)
        rhs = jnp.concatenate([blk(ops_ref[dr, OP_A_TRUE, rows, cols]), blk(gak_ref[i])], axis=1)
        wu = _dot(_bf(x_ref[i]), rhs)
        rw_ref[i, CHUNK:, :] = _bf(wu[:, :RW_GROUP])
        uloc_ref[i] = wu[:, RW_GROUP:]

    for i, (dr, grp, c) in enumerate(items):
        rows, cols = sl(c, grp)
        rhs = jnp.concatenate([blk(rw_ref[i, CHUNK:, :]), blk(_bf(uloc_ref[i]))], axis=1)
        gwu = _dot(grb_ref[i], rhs)
        gkv = _dot(grk_ref[i], blk(ops_ref[dr, OP_V, rows, cols]))
        rw_ref[i, :CHUNK, :] = _bf(rtrue_ref[dr, rows, cols] + gwu[:, :RW_GROUP])
        yloc_ref[i] = gwu[:, RW_GROUP:] + gkv

    bm_f = bm.astype(F32)
    for step in range(N_CHUNK):
        for dr in range(2):
            c = N_CHUNK - 1 - step if dr == 1 else step
            y_ref = streams[dr][3]
            for grp in range(RW_GROUPS):
                i = (dr * RW_GROUPS + grp) * N_CHUNK + c
                rows, cols = sl(c, grp)
                zt = zt_ref[dr, grp]
                rw = lax.dot_general(rw_ref[i], _bf(zt), (((1,), (1,)), ((), ())), preferred_element_type=F32)
                y_ref[rows, cols] = rw[:CHUNK] + yloc_ref[i]
                u = rw[CHUNK:] + uloc_ref[i]
                uv = jnp.concatenate([_bf(u), ops_ref[dr, OP_V, rows, cols]], axis=0)
                bk = jnp.concatenate([ops_ref[dr, OP_B_HAT, rows, cols], ops_ref[dr, OP_K_HAT, rows, cols]], axis=0)
                upd = lax.dot_general(uv, bk, (((0,), (0,)), ((), ())), preferred_element_type=F32)
                zt_ref[dr, grp] = zt * gam_ref[dr, c * CHUNK:c * CHUNK + 1, cols] + upd * bm_f


def _rwkv(proj, n_tiles, p):
    rf, rb = _proj_specs(n_tiles, COL_RW_R)
    kf, kb = _proj_specs(n_tiles, COL_RW_K)
    vf, vb = _proj_specs(n_tiles, COL_RW_V)
    of, ob = _out_specs(n_tiles)
    n_tok = proj.shape[0]
    kkf, kkb = _proj_specs(n_tiles, COL_RW_KK)
    consts = (p['rw_k_a'], p['rw_r_k'], p['ones64'],
              p['rw_w0'], p['rw_wa1'], p['rw_w2'], p['rw_a0'], p['rw_a2'], p['rw_g1'], p['rw_g2'])
    out = jax.ShapeDtypeStruct((n_tok, BRANCH_W), F32)
    return pl.pallas_call(
        _rwkv_kernel,
        grid=(n_tiles,),
        in_specs=[rf, kf, vf, kkf, rb, kb, vb, kkb] + [_full(c) for c in consts],
        out_specs=[of, ob, of, of],
        out_shape=[out, out, out, out],
        scratch_shapes=[
            pltpu.VMEM((2, RW_GROUPS, RW_GROUP, RW_GROUP), F32),
            pltpu.VMEM((2, 8, TILE, BRANCH_W), BF16),
            pltpu.VMEM((2, TILE, BRANCH_W), F32),
            pltpu.VMEM((2, TILE, BRANCH_W), F32),
            pltpu.VMEM((RW_ITEMS, CHUNK, RW_GROUP), F32),
            pltpu.VMEM((RW_ITEMS, CHUNK, RW_GROUP), F32),
            pltpu.VMEM((RW_ITEMS, CHUNK, RW_GROUP), BF16),
            pltpu.VMEM((RW_ITEMS, CHUNK, RW_GROUP), BF16),
            pltpu.VMEM((RW_ITEMS, CHUNK, RW_GROUP), BF16),
            pltpu.VMEM((RW_ITEMS, 2 * CHUNK, RW_GROUP), BF16),
            pltpu.VMEM((RW_ITEMS, CHUNK, RW_GROUP), F32),
            pltpu.VMEM((RW_ITEMS, CHUNK, RW_GROUP), F32),
        ],
        compiler_params=_cparams(),
        name="rwkv_scan",
    )(proj, proj, proj, proj, proj, proj, proj, proj, *consts)


def _merge_kernel(x_ref, mod_ref, u_ref, gg_ref, lg_ref,
                  s5f_ref, s5b_ref, glf_ref, glb_ref, lrf_ref, lrb_ref, rwf_ref, rwb_ref, bonus_ref, rgate_ref,
                  s5d_ref, glu_ref, gnw_ref, ones64_ref, ones128_ref, lnw_ref, lnb_ref,
                  wg_ref, bg_ref, wb_ref, wo_ref, ln1g_ref, ln1b_ref, o_ref, *, dn_alpha):
    m = _mod_rows(mod_ref, pl.program_id(0) == 0)
    x = x_ref[...]
    h = _bf(x * (1.0 + _mod_part(m, 1)) + _mod_part(m, 0))

    ya = _gelu(s5f_ref[...] + s5b_ref[...] + s5d_ref[...] * u_ref[...])
    ya = ya * _sigmoid(_mm(ya, glu_ref[...]))

    o = glf_ref[...] + glb_ref[...]
    ms = _seg_sum(o * o, ones128_ref[...]) * (1.0 / GLA_DV)
    yb = (o * lax.rsqrt(ms + LN_EPS)) * gnw_ref[...] * _silu(gg_ref[...])

    yc = (lrf_ref[...] + lrb_ref[...]) * _gelu(lg_ref[...])

    y = rwf_ref[...] + rwb_ref[...]
    mu = _seg_sum(y, ones64_ref[...]) * (1.0 / RW_N)
    yc0 = y - mu
    var = _seg_sum(yc0 * yc0, ones64_ref[...]) * (1.0 / RW_N)
    yn = (yc0 * lax.rsqrt(var + RW_GN_EPS)) * lnw_ref[...] + lnb_ref[...]
    yd = (yn + bonus_ref[...]) * rgate_ref[...]

    z = jnp.zeros((TILE, D_MODEL), F32)
    for n, yn_ in enumerate((ya, yb, yc, yd)):
        cols = slice(n * D_MODEL, (n + 1) * D_MODEL)
        gate = _sigmoid(jnp.dot(h, wg_ref[:, cols], preferred_element_type=F32) + bg_ref[:, cols])
        z = z + gate * _mm(yn_, wb_ref[n])
    out = _mm(z, wo_ref[...])
    x1 = dn_alpha * x + _mod_part(m, 2) * out
    o_ref[...] = _layer_norm(x1, LN_EPS) * ln1g_ref[...] + ln1b_ref[...]


def _merge(xs, mod, proj, parts, n_tiles, p, dn_alpha):
    n_tok = xs.shape[0]
    tile_d = pl.BlockSpec((TILE, D_MODEL), lambda i: (i, 0))
    tile_w = pl.BlockSpec((TILE, BRANCH_W), lambda i: (i, 0))

    def pcol(col):
        return pl.BlockSpec((TILE, BRANCH_W), lambda i: (i, col))

    consts = (p['s5_d'], p['s5_w_glu'], p['gla_norm_w'], p['ones64'], p['ones128'], p['rw_ln_w'], p['rw_ln_b'],
              p['w_gate'], p['b_gate'], p['w_branch'], p['w_out'], p['ln1_g'], p['ln1_b'])
    return pl.pallas_call(
        functools.partial(_merge_kernel, dn_alpha=dn_alpha),
        grid=(n_tiles,),
        in_specs=[tile_d, _full(mod), pcol(COL_S5_U), pcol(COL_GLA_G), pcol(COL_LRU_GATE)]
                 + [tile_w] * len(parts) + [_full(c) for c in consts],
        out_specs=tile_d,
        out_shape=jax.ShapeDtypeStruct((n_tok, D_MODEL), F32),
        compiler_params=_cparams(),
        name="merge_ln1",
    )(xs, mod, proj, proj, proj, *parts, *consts)


def _ffn_kernel(x_ref, mod_ref, wup_ref, cw_ref, cb_ref, wdn_ref, ln2g_ref, ln2b_ref, o_ref, *,
                dn_alpha, first_tile):
    is_ctx = pl.program_id(0) + first_tile == 0
    m = _mod_rows(mod_ref, is_ctx)
    x = x_ref[...]
    h = _bf(x * (1.0 + _mod_part(m, 4)) + _mod_part(m, 3))
    acc = jnp.zeros((TILE, D_MODEL), F32)
    masks_by_width = {w: _shift_masks((TILE, w), (-1, 1), is_ctx) for w in set(FFN_CHUNKS)}
    start = 0
    for width in FFN_CHUNKS:
        ucols = slice(start, start + width)
        vcols = slice(FFN_HIDDEN + start, FFN_HIDDEN + start + width)
        start += width
        masks = masks_by_width[width]
        u = jnp.dot(h, wup_ref[:, ucols], preferred_element_type=F32)
        v = jnp.dot(h, wup_ref[:, vcols], preferred_element_type=F32)
        uc = (cb_ref[:, ucols] + cw_ref[1:2, ucols] * u
              + cw_ref[0:1, ucols] * _shift_rows(u, -1, masks) + cw_ref[2:3, ucols] * _shift_rows(u, 1, masks))
        acc = acc + _mm(_gelu(uc) * v, wdn_ref[ucols, :])
    x2 = dn_alpha * x + _mod_part(m, 5) * acc
    o_ref[...] = _layer_norm(x2, LN_EPS) * ln2g_ref[...] + ln2b_ref[...]


def _ffn(xs, mod, p, dn_alpha, skip_ctx):
    first = 1 if skip_ctx else 0
    n_out = xs.shape[0] - first * TILE
    consts = (p['ffn_w_up'], p['ffn_conv_w'], p['ffn_conv_b'], p['ffn_w_down'], p['ln2_g'], p['ln2_b'])
    return pl.pallas_call(
        functools.partial(_ffn_kernel, dn_alpha=dn_alpha, first_tile=first),
        grid=(n_out // TILE,),
        in_specs=[pl.BlockSpec((TILE, D_MODEL), lambda i: (i + first, 0)), _full(mod)] + [_full(c) for c in consts],
        out_specs=pl.BlockSpec((TILE, D_MODEL), lambda i: (i, 0)),
        out_shape=jax.ShapeDtypeStruct((n_out, D_MODEL), F32),
        compiler_params=_cparams(),
        name="convffn_ln2",
    )(xs, mod, *consts)


def _block_diag(blocks):
    n, a, b = blocks.shape
    eye = jnp.eye(n, dtype=blocks.dtype)
    return (eye[:, None, :, None] * blocks[:, :, None, :]).reshape(n * a, n * b)


def _pad_cols(w, n):
    return jnp.pad(w, [(0, 0)] * (w.ndim - 1) + [(0, n - w.shape[-1])])


def _pad_rows(w, n):
    return jnp.pad(w, [(0, 0)] * (w.ndim - 2) + [(0, n - w.shape[-2]), (0, 0)])


def _s5_perms():
    perm = np.zeros((2, TILE, TILE), np.float32)
    for j in range(S5_STEPS):
        for s in range(S5_SUB):
            t = s * S5_STEPS + j
            perm[0, j * S5_SUB + s, t] = 1.0
            perm[1, j * S5_SUB + s, TILE - 1 - t] = 1.0
    return jnp.asarray(perm, BF16), jnp.asarray(np.transpose(perm, (0, 2, 1)), BF16)


def _ones_blocks(width):
    idx = np.arange(BRANCH_W) // width
    return jnp.asarray(idx[:, None] == idx[None, :], BF16)


def _layer_params(q, i):
    p = {}
    row = lambda v: v[i].reshape(1, -1)
    dt = jnp.exp(q['s5_log_step'][i])[:, :, None]
    lam_re, lam_im = q['s5_lam_re'][i], q['s5_lam_im'][i]
    mag = jnp.exp(lam_re * dt)
    a_re, a_im = mag * jnp.cos(lam_im * dt), mag * jnp.sin(lam_im * dt)
    den = lam_re * lam_re + lam_im * lam_im
    f_re = ((a_re - 1.0) * lam_re + a_im * lam_im) / den
    f_im = (a_im * lam_re - (a_re - 1.0) * lam_im) / den
    b_re, b_im = q['s5_b_re'][i], q['s5_b_im'][i]
    d_re = f_re[..., None] * b_re[None] - f_im[..., None] * b_im[None]
    d_im = f_re[..., None] * b_im[None] + f_im[..., None] * b_re[None]
    gh = S5_G // 2
    halves = [slice(0, gh), slice(gh, S5_G)]
    bd = lambda blocks: _block_diag(jnp.swapaxes(blocks, 1, 2))
    p['s5_drive'] = jnp.stack([jnp.stack([jnp.concatenate([bd(d_re[dr, hs]), bd(d_im[dr, hs])], axis=1)
                                          for hs in halves]) for dr in range(2)]).astype(BF16)
    p['s5_a'] = jnp.concatenate([jnp.concatenate([a_re[:, hs].reshape(2, 1, -1), a_im[:, hs].reshape(2, 1, -1)],
                                                 axis=-1) for hs in halves], axis=-1)
    c_re, c_im = q['s5_c_re'][i], q['s5_c_im'][i]
    p['s5_read'] = jnp.stack([jnp.concatenate([bd(c_re[hs]), -bd(c_im[hs])], axis=0)
                              for hs in halves]).astype(BF16)
    p['s5_perm'], p['s5_perm_t'] = _s5_perms()
    p['s5_d'] = row(q['s5_d'])
    p['s5_w_glu'] = q['s5_w_glu'][i].astype(BF16)
    p['gla_w1'] = _pad_cols(q['gla_w1'][i], LORA_PAD).astype(BF16)
    p['gla_w2'] = _pad_rows(q['gla_w2'][i], LORA_PAD).astype(BF16)
    p['gla_bias'] = q['gla_bias'][i]
    p['gla_norm_w'] = jnp.tile(q['gla_norm_w'][i], GLA_HEADS).reshape(1, -1)
    p['lru_conv_w'] = q['lru_conv_w'][i]
    p['lru_conv_b'] = row(q['lru_conv_b'])
    p['lru_wa'] = jnp.stack([_block_diag(q['lru_w_a'][i, dr]) for dr in range(2)]).astype(BF16)
    p['lru_wx'] = jnp.stack([_block_diag(q['lru_w_x'][i, dr]) for dr in range(2)]).astype(BF16)
    p['lru_b_a'], p['lru_b_x'] = q['lru_b_a'][i], q['lru_b_x'][i]
    p['lru_c'] = -LRU_C * jax.nn.softplus(-q['lru_lam'][i])
    p['rw_mu'] = q['rw_mu'][i]
    for name in ('rw_k_k', 'rw_k_a', 'rw_r_k', 'rw_ln_w', 'rw_ln_b'):
        p[name] = row(q[name])
    p['rw_w0'], p['rw_a0'] = q['rw_w0'][i], q['rw_a0'][i]
    p['rw_wa1'] = jnp.concatenate([_pad_cols(q['rw_w1'][i], LORA_PAD), _pad_cols(q['rw_a1'][i], LORA_PAD)],
                                  axis=-1).astype(BF16)
    p['rw_w2'] = _pad_rows(q['rw_w2'][i], LORA_PAD).astype(BF16)
    p['rw_a2'] = _pad_rows(q['rw_a2'][i], LORA_PAD).astype(BF16)
    p['rw_g1'] = q['rw_g1'][i].astype(BF16)
    p['rw_g2'] = q['rw_g2'][i].astype(BF16)
    p['ones64'] = _ones_blocks(RW_N)
    p['ones128'] = _ones_blocks(GLA_DV)
    p['w_in'] = q['w_in'][i].astype(BF16)
    p['w_gate'] = q['w_gate'][i].astype(BF16)
    p['b_gate'] = row(q['b_gate'])
    p['w_branch'] = q['w_branch'][i].astype(BF16)
    p['w_out'] = q['w_out'][i].astype(BF16)
    p['ffn_w_up'] = q['ffn_w_up'][i].astype(BF16)
    p['ffn_w_down'] = q['ffn_w_down'][i].astype(BF16)
    p['ffn_conv_w'] = q['ffn_conv_w'][i]
    p['ffn_conv_b'] = row(q['ffn_conv_b'])
    for name in ('ln1_g', 'ln1_b', 'ln2_g', 'ln2_b'):
        p[name] = row(q[name])
    return p


def kernel(x, c, ctx, c_ctx, w_mod, b_mod, w_in, s5_lam_re, s5_lam_im, s5_log_step, s5_b_re, s5_b_im, s5_c_re, s5_c_im, s5_d, s5_w_glu, gla_w1, gla_w2, gla_bias, gla_norm_w, lru_conv_w, lru_conv_b, lru_w_a, lru_b_a, lru_w_x, lru_b_x, lru_lam, rw_mu, rw_w0, rw_w1, rw_w2, rw_a0, rw_a1, rw_a2, rw_g1, rw_g2, rw_k_k, rw_k_a, rw_r_k, rw_ln_w, rw_ln_b, w_branch, w_gate, b_gate, w_out, ln1_g, ln1_b, ln2_g, ln2_b, ffn_w_up, ffn_conv_w, ffn_conv_b, ffn_w_down):
    q = dict(w_in=w_in, s5_lam_re=s5_lam_re, s5_lam_im=s5_lam_im, s5_log_step=s5_log_step, s5_b_re=s5_b_re,
             s5_b_im=s5_b_im, s5_c_re=s5_c_re, s5_c_im=s5_c_im, s5_d=s5_d, s5_w_glu=s5_w_glu,
             gla_w1=gla_w1, gla_w2=gla_w2, gla_bias=gla_bias, gla_norm_w=gla_norm_w,
             lru_conv_w=lru_conv_w, lru_conv_b=lru_conv_b, lru_w_a=lru_w_a, lru_b_a=lru_b_a,
             lru_w_x=lru_w_x, lru_b_x=lru_b_x, lru_lam=lru_lam,
             rw_mu=rw_mu, rw_w0=rw_w0, rw_w1=rw_w1, rw_w2=rw_w2, rw_a0=rw_a0, rw_a1=rw_a1, rw_a2=rw_a2,
             rw_g1=rw_g1, rw_g2=rw_g2, rw_k_k=rw_k_k, rw_k_a=rw_k_a, rw_r_k=rw_r_k,
             rw_ln_w=rw_ln_w, rw_ln_b=rw_ln_b,
             w_branch=w_branch, w_gate=w_gate, b_gate=b_gate, w_out=w_out,
             ln1_g=ln1_g, ln1_b=ln1_b, ln2_g=ln2_g, ln2_b=ln2_b,
             ffn_w_up=ffn_w_up, ffn_conv_w=ffn_conv_w, ffn_conv_b=ffn_conv_b, ffn_w_down=ffn_w_down)
    batch, seq, d = x.shape
    assert batch == 1 and d == D_MODEL and ctx.shape[1] == CTX_LEN and seq % TILE == 0
    depth = w_mod.shape[0]
    dn_alpha = (2 * depth) ** 0.25
    n_tiles = 1 + seq // TILE

    cc = jnp.zeros((8, D_MODEL), F32).at[0].set(c[0]).at[1].set(c_ctx)
    mods = _modulation(cc, w_mod, b_mod)
    xs = _ln0(ctx[0], x[0])
    for i in range(depth):
        p = _layer_params(q, i)
        mod = mods[i]
        proj = _inproj(xs, mod, p)
        s5f, s5b = _s5(proj, n_tiles, p)
        glf, glb = _gla(proj, n_tiles, p)
        lrf, lrb = _lru(proj, n_tiles, p)
        rwf, rwb, bonus, rgate = _rwkv(proj, n_tiles, p)
        xs = _merge(xs, mod, proj, (s5f, s5b, glf, glb, lrf, lrb, rwf, rwb, bonus, rgate), n_tiles, p, dn_alpha)
        xs = _ffn(xs, mod, p, dn_alpha, skip_ctx=i == depth - 1)
    return xs.reshape(batch, seq, d)
```

```python
import functools
import math

import numpy as np
import jax
import jax.numpy as jnp
from jax import lax
from jax.experimental import pallas as pl
from jax.experimental.pallas import tpu as pltpu

F32 = jnp.float32
BF16 = jnp.bfloat16

D_MODEL = 1024
GRID_W = 64
CTX_LEN = 256
TILE = CTX_LEN
BRANCH_W = 512
N_BRANCH = 4

S5_H, S5_P = 16, 64
S5_G = BRANCH_W // S5_H
S5_STATE = S5_G * S5_P
S5_SUB = 8
S5_STEPS = TILE // S5_SUB
S5_UNROLL = True

GLA_HEADS, GLA_DK = 4, 64
GLA_DV = BRANCH_W // GLA_HEADS
GLA_QK = GLA_HEADS * GLA_DK
GLA_TAU = 16.0
CHUNK = 64
N_CHUNK = TILE // CHUNK

LRU_C = 8.0
LRU_CONV_LEFT = 2

RW_N = 64
RW_GROUP = 256
RW_GN_EPS = 64e-5
RW_DECAY_SCALE = math.exp(-0.5)

FFN_HIDDEN = 2816
FFN_CHUNKS = (1024, 1024, 768)
assert sum(FFN_CHUNKS) == FFN_HIDDEN
LORA_PAD = 128

LN_EPS = 1e-6
SUBLANES = 8
V7X_VMEM_LIMIT_BYTES = 56 * 1024 * 1024


def _bf(x):
    return x.astype(BF16)


def _mm(a, b):
    return jnp.dot(_bf(a), _bf(b), preferred_element_type=F32)


def _dot(a, b):
    return jnp.dot(a, b, preferred_element_type=F32)


def _mm_nt(a, b):
    return lax.dot_general(_bf(a), _bf(b), (((1,), (1,)), ((), ())), preferred_element_type=F32)


def _mm_tn(a, b):
    return lax.dot_general(_bf(a), _bf(b), (((0,), (0,)), ((), ())), preferred_element_type=F32)


def _split2(x):
    hi = _bf(x)
    lo = _bf(x - hi.astype(F32))
    return hi, lo


def _mm_hp(a, b):
    ah, al = _split2(a)
    bh, bl = _split2(b)
    dot = functools.partial(jnp.dot, preferred_element_type=F32)
    return dot(ah, bh) + (dot(ah, bl) + dot(al, bh))


def _seg_sum(x, ones_blk):
    return _dot(_bf(x), ones_blk)


def _iota(shape, dim):
    return lax.broadcasted_iota(jnp.int32, shape, dim)


def _sigmoid(x):
    return 0.5 * jnp.tanh(0.5 * x) + 0.5


def _silu(x):
    return x * _sigmoid(x)


def _gelu(x):
    return 0.5 * x * (1.0 + jnp.tanh(math.sqrt(2.0 / math.pi) * (x + 0.044715 * (x * x * x))))


def _log_sigmoid(x):
    return jnp.minimum(x, 0.0) - jnp.log1p(jnp.exp(-jnp.abs(x)))


def _neg_expm1_twice(y):
    t = jnp.tanh(y)
    return (-2.0 * t) / (1.0 - t)


def _layer_norm(x, eps):
    mu = jnp.mean(x, axis=-1, keepdims=True)
    xc = x - mu
    var = jnp.mean(xc * xc, axis=-1, keepdims=True)
    return xc * lax.rsqrt(var + eps)


def _shift_masks(shape, shifts, is_ctx):
    n = shape[0]
    t = _iota(shape, 0)
    pos = jnp.where(is_ctx, t, t & (GRID_W - 1))
    row_len = jnp.where(is_ctx, n, GRID_W)
    return {k: ((pos + k >= 0) & (pos + k < row_len)).astype(F32) for k in shifts}


def _shift_rows(x, k, masks):
    return pltpu.roll(x, (-k) % x.shape[0], axis=0) * masks[k]


def _seg_cumsum(x, reverse):
    n = x.shape[0]
    t = _iota(x.shape, 0) & (CHUNK - 1)
    d = 1
    while d < CHUNK:
        if reverse:
            x = x + jnp.where(t < CHUNK - d, pltpu.roll(x, n - d, axis=0), 0.0)
        else:
            x = x + jnp.where(t >= d, pltpu.roll(x, d, axis=0), 0.0)
        d *= 2
    return x


def _scan_groups(a, b, reverse):
    n = a.shape[0]
    t = _iota(a.shape, 0) & (SUBLANES - 1)
    d = 1
    while d < SUBLANES:
        if reverse:
            valid, sh = t < SUBLANES - d, n - d
        else:
            valid, sh = t >= d, d
        a_s = jnp.where(valid, pltpu.roll(a, sh, axis=0), 1.0)
        b_s = jnp.where(valid, pltpu.roll(b, sh, axis=0), 0.0)
        b = b + a * b_s
        a = a * a_s
        d *= 2
    return a, b


def _chain_groups(scans):
    n_grp = scans[0][0].shape[0] // SUBLANES
    parts = [[None] * n_grp for _ in scans]
    carries = [s[2] for s in scans]
    for step in range(n_grp):
        for i, (a, b, _, reverse) in enumerate(scans):
            gi = n_grp - 1 - step if reverse else step
            rows = slice(gi * SUBLANES, (gi + 1) * SUBLANES)
            h = b[rows] + a[rows] * carries[i]
            parts[i][gi] = h
            last = 0 if reverse else SUBLANES - 1
            carries[i] = h[last:last + 1]
    return [(jnp.concatenate(p, axis=0), c) for p, c in zip(parts, carries)]


def _chunk_rows(x, row):
    parts = [jnp.broadcast_to(x[c * CHUNK + row:c * CHUNK + row + 1], (CHUNK, x.shape[1]))
             for c in range(x.shape[0] // CHUNK)]
    return jnp.concatenate(parts, axis=0)


def _tile4(x):
    return jnp.concatenate([x] * (RW_GROUP // CHUNK), axis=0)


def _block_mask(rows, cols, row_blk, col_blk):
    return (_iota((rows, cols), 0) // row_blk) == (_iota((rows, cols), 1) // col_blk)


def _causal_cat(reverse, strict):
    t = _iota((CHUNK, RW_GROUP), 0)
    s = _iota((CHUNK, RW_GROUP), 1) & (CHUNK - 1)
    if reverse:
        return (s > t) if strict else (s >= t)
    return (s < t) if strict else (s <= t)


def _bwd_tile(i, n_tiles):
    return jnp.where(i == 0, 0, n_tiles - i)


def _cparams():
    return pltpu.CompilerParams(dimension_semantics=("arbitrary",), vmem_limit_bytes=V7X_VMEM_LIMIT_BYTES)


def _full(arr):
    nd = arr.ndim
    return pl.BlockSpec(arr.shape, lambda i, _nd=nd: (0,) * _nd)


def _ln0_kernel(ctx_ref, x_ref, o_ref):
    is_ctx = pl.program_id(0) == 0
    o_ref[...] = _layer_norm(jnp.where(is_ctx, ctx_ref[...], x_ref[...]), LN_EPS)


def _ln0(ctx2d, x2d):
    n_tok = ctx2d.shape[0] + x2d.shape[0]
    return pl.pallas_call(
        _ln0_kernel,
        grid=(n_tok // TILE,),
        in_specs=[pl.BlockSpec((TILE, D_MODEL), lambda i: (0, 0)),
                  pl.BlockSpec((TILE, D_MODEL), lambda i: (jnp.maximum(i - 1, 0), 0))],
        out_specs=pl.BlockSpec((TILE, D_MODEL), lambda i: (i, 0)),
        out_shape=jax.ShapeDtypeStruct((n_tok, D_MODEL), F32),
        compiler_params=_cparams(),
        name="ln0",
    )(ctx2d, x2d)


MOD_COLS = 1536


def _mod_kernel(c_ref, w_ref, b_ref, o_ref):
    s = _silu(c_ref[...])
    o_ref[0] = _mm_hp(s, w_ref[0]) + b_ref[0]


def _modulation(cc, w_mod, b_mod):
    depth = w_mod.shape[0]
    n_col = w_mod.shape[2] // MOD_COLS
    return pl.pallas_call(
        _mod_kernel,
        grid=(depth, n_col),
        in_specs=[pl.BlockSpec((8, D_MODEL), lambda l, j: (0, 0)),
                  pl.BlockSpec((1, D_MODEL, MOD_COLS), lambda l, j: (l, 0, j)),
                  pl.BlockSpec((1, 1, MOD_COLS), lambda l, j: (l, 0, j))],
        out_specs=pl.BlockSpec((1, 8, MOD_COLS), lambda l, j: (l, 0, j)),
        out_shape=jax.ShapeDtypeStruct((depth, 8, w_mod.shape[2]), F32),
        compiler_params=pltpu.CompilerParams(dimension_semantics=("arbitrary", "arbitrary"),
                                             vmem_limit_bytes=V7X_VMEM_LIMIT_BYTES),
        name="modulation",
    )(cc, w_mod, b_mod.reshape(depth, 1, -1))


def _mod_rows(mod_ref, is_ctx):
    m = mod_ref[...]
    return jnp.where(is_ctx, m[1:2], m[0:1])


def _mod_part(m, j):
    return m[:, j * D_MODEL:(j + 1) * D_MODEL]


(COL_S5_U, COL_GLA_QK, COL_GLA_V, COL_GLA_G, COL_LRU_X, COL_LRU_GATE, COL_RW_R, COL_RW_K, COL_RW_V,
 COL_RW_KK) = range(10)
N_PROJ_COLS = 10 * BRANCH_W


def _col(block):
    return slice(block * BRANCH_W, (block + 1) * BRANCH_W)


def _inproj_kernel(x_ref, mod_ref, w_ref, cw_ref, cb_ref, mu_ref, kkw_ref, ones_ref, o_ref):
    is_ctx = pl.program_id(0) == 0
    m = _mod_rows(mod_ref, is_ctx)
    h = _bf(x_ref[...] * (1.0 + _mod_part(m, 1)) + _mod_part(m, 0))
    n_taps = cw_ref.shape[0]
    taps = [k - LRU_CONV_LEFT for k in range(n_taps)]
    masks = _shift_masks((TILE, BRANCH_W), sorted(set(taps + [-1, 1]) - {0}), is_ctx)

    x = _dot(h, w_ref[:, _col(COL_LRU_X)])
    o_ref[:, _col(COL_LRU_X)] = cb_ref[...] + sum(
        cw_ref[j:j + 1] * (_shift_rows(x, k, masks) if k else x) for j, k in enumerate(taps))

    for j, block in enumerate((COL_RW_R, COL_RW_K, COL_RW_V)):
        x = _dot(h, w_ref[:, _col(block)])
        xs = 0.5 * (_shift_rows(x, -1, masks) + _shift_rows(x, 1, masks))
        x = x + mu_ref[j:j + 1] * (xs - x)
        o_ref[:, _col(block)] = x
        if block == COL_RW_K:
            kk = x * kkw_ref[...]
            o_ref[:, _col(COL_RW_KK)] = kk / jnp.maximum(jnp.sqrt(_seg_sum(kk * kk, ones_ref[...])), 1e-12)

    plain = slice(0, COL_LRU_X * BRANCH_W)
    o_ref[:, plain] = _dot(h, w_ref[:, plain])
    o_ref[:, _col(COL_LRU_GATE)] = _dot(h, w_ref[:, _col(COL_LRU_GATE)])


def _inproj(xs, mod, p):
    n_tok = xs.shape[0]
    consts = (p['w_in'], p['lru_conv_w'], p['lru_conv_b'], p['rw_mu'], p['rw_k_k'], p['ones64'])
    return pl.pallas_call(
        _inproj_kernel,
        grid=(n_tok // TILE,),
        in_specs=[pl.BlockSpec((TILE, D_MODEL), lambda i: (i, 0)), _full(mod)] + [_full(c) for c in consts],
        out_specs=pl.BlockSpec((TILE, N_PROJ_COLS), lambda i: (i, 0)),
        out_shape=jax.ShapeDtypeStruct((n_tok, N_PROJ_COLS), F32),
        compiler_params=_cparams(),
        name="inproj",
    )(xs, mod, *consts)


def _proj_specs(n_tiles, col):
    fwd = pl.BlockSpec((TILE, BRANCH_W), lambda i: (i, col))
    bwd = pl.BlockSpec((TILE, BRANCH_W), lambda i: (_bwd_tile(i, n_tiles), col))
    return fwd, bwd


def _out_specs(n_tiles):
    fwd = pl.BlockSpec((TILE, BRANCH_W), lambda i: (i, 0))
    bwd = pl.BlockSpec((TILE, BRANCH_W), lambda i: (_bwd_tile(i, n_tiles), 0))
    return fwd, bwd


def _lru_kernel(xf_ref, xb_ref, wa_ref, ba_ref, wx_ref, bx_ref, lc_ref,
                hf_ref, hb_ref, carry_ref):
    @pl.when(pl.program_id(0) == 0)
    def _():
        carry_ref[...] = jnp.zeros_like(carry_ref)

    scans = []
    for dr, x_ref in enumerate((xf_ref, xb_ref)):
        xc = x_ref[...]
        r = _sigmoid(_mm(xc, wa_ref[dr]) + ba_ref[dr:dr + 1])
        ig = _sigmoid(_mm(xc, wx_ref[dr]) + bx_ref[dr:dr + 1])
        log_a = lc_ref[dr:dr + 1] * r
        a = jnp.exp(log_a)
        b = jnp.sqrt(_neg_expm1_twice(log_a)) * (ig * xc)
        scans.append(_scan_groups(a, b, reverse=dr == 1) + (carry_ref[dr:dr + 1], dr == 1))
    for dr, ((h, carry), o_ref) in enumerate(zip(_chain_groups(scans), (hf_ref, hb_ref))):
        o_ref[...] = h
        carry_ref[dr:dr + 1] = carry


def _lru(proj, n_tiles, p):
    xf, xb = _proj_specs(n_tiles, COL_LRU_X)
    of, ob = _out_specs(n_tiles)
    n_tok = proj.shape[0]
    consts = (p['lru_wa'], p['lru_b_a'], p['lru_wx'], p['lru_b_x'], p['lru_c'])
    out = jax.ShapeDtypeStruct((n_tok, BRANCH_W), F32)
    return pl.pallas_call(
        _lru_kernel,
        grid=(n_tiles,),
        in_specs=[xf, xb] + [_full(c) for c in consts],
        out_specs=[of, ob],
        out_shape=[out, out],
        scratch_shapes=[pltpu.VMEM((8, BRANCH_W), F32)],
        compiler_params=_cparams(),
        name="lru_scan",
    )(proj, proj, *consts)


S5_HALF = S5_STATE // 2


def _cmul(ar, ai, br, bi):
    return ar * br - ai * bi, ar * bi + ai * br


def _s5_cols(half):
    base = 2 * S5_HALF * half
    return slice(base, base + S5_HALF), slice(base + S5_HALF, base + 2 * S5_HALF)


def _s5_kernel(uf_ref, ub_ref, perm_ref, permt_ref, a_ref, drive_ref, read_ref,
               yf_ref, yb_ref, h_ref, carry_ref, pw_ref):
    @pl.when(pl.program_id(0) == 0)
    def _():
        carry_ref[...] = jnp.zeros_like(carry_ref)
        for dr in range(2):
            for half in range(2):
                re, im = _s5_cols(half)
                a_re, a_im = a_ref[dr, :, re], a_ref[dr, :, im]
                q_re, q_im = a_re, a_im
                for j in range(S5_STEPS):
                    rows = slice(j * S5_SUB, (j + 1) * S5_SUB)
                    pw_ref[dr, rows, re] = jnp.broadcast_to(q_re, (S5_SUB, S5_HALF))
                    pw_ref[dr, rows, im] = jnp.broadcast_to(q_im, (S5_SUB, S5_HALF))
                    q_re, q_im = _cmul(q_re, q_im, a_re, a_im)

    sub = _iota((S5_SUB, S5_HALF), 0)
    half_u = BRANCH_W // 2
    u_refs, o_refs = (uf_ref, ub_ref), (yf_ref, yb_ref)
    u_perm = [_bf(_dot(perm_ref[dr], _bf(u_refs[dr][...]))) for dr in range(2)]

    def state_cols(half):
        return slice(2 * S5_HALF * half, 2 * S5_HALF * (half + 1))

    def drive(dr, half):
        h_ref[dr, :, state_cols(half)] = _dot(u_perm[dr][:, half * half_u:(half + 1) * half_u], drive_ref[dr, half])

    def scan(dr, half):
        re, im = _s5_cols(half)
        a_re = a_ref[dr, :, re]
        a_im = a_ref[dr, :, im]
        a_re8 = jnp.broadcast_to(a_re, (S5_SUB, S5_HALF))
        a_im8 = jnp.broadcast_to(a_im, (S5_SUB, S5_HALF))
        h_re = jnp.zeros((S5_SUB, S5_HALF), F32)
        h_im = jnp.zeros((S5_SUB, S5_HALF), F32)
        for j in range(S5_STEPS):
            rows = slice(j * S5_SUB, (j + 1) * S5_SUB)
            m_re, m_im = _cmul(a_re8, a_im8, h_re, h_im)
            h_re = m_re + h_ref[dr, rows, re]
            h_im = m_im + h_ref[dr, rows, im]
            h_ref[dr, rows, re] = h_re
            h_ref[dr, rows, im] = h_im
        p_re, p_im = a_re, a_im
        for _ in range(int(math.log2(S5_STEPS))):
            p_re, p_im = _cmul(p_re, p_im, p_re, p_im)
        s_re = carry_ref[dr, :, re]
        s_im = carry_ref[dr, :, im]
        in_re = jnp.zeros((S5_SUB, S5_HALF), F32)
        in_im = jnp.zeros((S5_SUB, S5_HALF), F32)
        for s in range(S5_SUB):
            in_re = jnp.where(sub == s, s_re, in_re)
            in_im = jnp.where(sub == s, s_im, in_im)
            m_re, m_im = _cmul(p_re, p_im, s_re, s_im)
            s_re = m_re + h_re[s:s + 1]
            s_im = m_im + h_im[s:s + 1]
        carry_ref[dr, :, re] = s_re
        carry_ref[dr, :, im] = s_im
        for j in range(S5_STEPS):
            rows = slice(j * S5_SUB, (j + 1) * S5_SUB)
            c_re, c_im = _cmul(pw_ref[dr, rows, re], pw_ref[dr, rows, im], in_re, in_im)
            h_ref[dr, rows, re] = h_ref[dr, rows, re] + c_re
            h_ref[dr, rows, im] = h_ref[dr, rows, im] + c_im

    def readout(dr):
        y_p = jnp.concatenate([_mm(h_ref[dr, :, state_cols(half)], read_ref[half]) for half in range(2)], axis=1)
        hi, lo = _split2(y_p)
        o_refs[dr][...] = _dot(permt_ref[dr], hi) + _dot(permt_ref[dr], lo)

    for dr in range(2):
        for half in range(2):
            drive(dr, half)
    for dr in range(2):
        for half in range(2):
            scan(dr, half)
        readout(dr)


def _s5(proj, n_tiles, p):
    uf, ub = _proj_specs(n_tiles, COL_S5_U)
    of, ob = _out_specs(n_tiles)
    n_tok = proj.shape[0]
    consts = (p['s5_perm'], p['s5_perm_t'], p['s5_a'], p['s5_drive'], p['s5_read'])
    out = jax.ShapeDtypeStruct((n_tok, BRANCH_W), F32)
    return pl.pallas_call(
        _s5_kernel,
        grid=(n_tiles,),
        in_specs=[uf, ub] + [_full(c) for c in consts],
        out_specs=[of, ob],
        out_shape=[out, out],
        scratch_shapes=[pltpu.VMEM((2, TILE, 2 * S5_STATE), F32), pltpu.VMEM((2, 1, 2 * S5_STATE), F32),
                        pltpu.VMEM((2, TILE, 2 * S5_STATE), F32)],
        compiler_params=_cparams(),
        name="s5_scan",
    )(proj, proj, *consts)


def _gla_kernel(qkf_ref, vf_ref, qkb_ref, vb_ref, w1_ref, w2_ref, bias_ref, of_ref, ob_ref, st_ref):
    @pl.when(pl.program_id(0) == 0)
    def _():
        st_ref[...] = jnp.zeros_like(st_ref)

    bm_k = _bf(_block_mask(RW_GROUP, GLA_QK, CHUNK, GLA_DK).astype(F32))
    bm_v = _bf(_block_mask(RW_GROUP, BRANCH_W, CHUNK, GLA_DV).astype(F32))
    bm_s = _block_mask(BRANCH_W, GLA_QK, GLA_DV, GLA_DK)
    streams = ((qkf_ref, vf_ref, of_ref), (qkb_ref, vb_ref, ob_ref))
    prep = []
    for dr, (qk_ref, v_ref, _) in enumerate(streams):
        reverse = dr == 1
        q = qk_ref[:, :GLA_QK] * (GLA_DK ** -0.5)
        k = qk_ref[:, GLA_QK:]
        la = _log_sigmoid(_mm(_mm(k, w1_ref[dr]), w2_ref[dr]) + bias_ref[dr:dr + 1]) * (1.0 / GLA_TAU)
        b = _seg_cumsum(la, reverse)
        b_end = _chunk_rows(b, 0 if reverse else CHUNK - 1)
        b_mid = _chunk_rows(b, CHUNK - 1 - CHUNK // 2 if reverse else CHUNK // 2)
        prep.append(dict(qe=_bf(q * jnp.exp(b - b_mid)), ke=_bf(k * jnp.exp(b_mid - b)),
                         kl=_bf(k * jnp.exp(b_end - b)), qb=_bf(q * jnp.exp(b)), decay=jnp.exp(b_end),
                         v=_bf(v_ref[...]), causal=_causal_cat(reverse, strict=False)))
    states = [st_ref[0], st_ref[1]]
    for step in range(N_CHUNK):
        for dr, (_, _, o_ref) in enumerate(streams):
            c = N_CHUNK - 1 - step if dr == 1 else step
            rows = slice(c * CHUNK, (c + 1) * CHUNK)
            t = prep[dr]
            ke_blk = _tile4(t['ke'][rows]) * bm_k
            att = jnp.where(t['causal'], _mm_nt(t['qe'][rows], ke_blk), 0.0)
            v_blk = _tile4(t['v'][rows]) * bm_v
            o_ref[rows, :] = _mm(att, v_blk) + _mm_nt(t['qb'][rows], states[dr])
            states[dr] = (states[dr] * t['decay'][c * CHUNK:c * CHUNK + 1]
                          + jnp.where(bm_s, _mm_tn(t['v'][rows], t['kl'][rows]), 0.0))
    st_ref[0], st_ref[1] = states


def _gla(proj, n_tiles, p):
    qkf, qkb = _proj_specs(n_tiles, COL_GLA_QK)
    vf, vb = _proj_specs(n_tiles, COL_GLA_V)
    of, ob = _out_specs(n_tiles)
    n_tok = proj.shape[0]
    consts = (p['gla_w1'], p['gla_w2'], p['gla_bias'])
    out = jax.ShapeDtypeStruct((n_tok, BRANCH_W), F32)
    return pl.pallas_call(
        _gla_kernel,
        grid=(n_tiles,),
        in_specs=[qkf, vf, qkb, vb] + [_full(c) for c in consts],
        out_specs=[of, ob],
        out_shape=[out, out],
        scratch_shapes=[pltpu.VMEM((2, BRANCH_W, GLA_QK), F32)],
        compiler_params=_cparams(),
        name="gla_scan",
    )(proj, proj, proj, proj, *consts)


RW_GROUPS = BRANCH_W // RW_GROUP
RW_ITEMS = 2 * RW_GROUPS * N_CHUNK
OP_A_T, OP_R_T, OP_B_T, OP_K_T, OP_A_TRUE, OP_B_HAT, OP_K_HAT, OP_V = range(8)


def _rwkv_kernel(rf_ref, kf_ref, vf_ref, kkf_ref, rb_ref, kb_ref, vb_ref, kkb_ref,
                 ka_ref, rk_ref, ones_ref,
                 w0_ref, wa1_ref, w2_ref, a0_ref, a2_ref, g1_ref, g2_ref,
                 yf_ref, yb_ref, bonus_ref, gate_ref,
                 zt_ref, ops_ref, rtrue_ref, gam_ref, l_ref, x_ref, gak_ref, grb_ref, grk_ref,
                 rw_ref, yloc_ref, uloc_ref):
    is_ctx = pl.program_id(0) == 0

    @pl.when(is_ctx)
    def _():
        zt_ref[...] = jnp.zeros_like(zt_ref)

    streams = ((rf_ref, kf_ref, vf_ref, yf_ref), (rb_ref, kb_ref, vb_ref, yb_ref))
    for dr, (r_ref, k_ref, v_ref, _) in enumerate(streams):
        reverse = dr == 1
        r, k, v = r_ref[...], k_ref[...], v_ref[...]
        kk = (kkb_ref if reverse else kkf_ref)[...]
        if dr == 0:
            bonus_ref[...] = _seg_sum(r * k * rk_ref[...], ones_ref[...]) * v
            gate_ref[...] = _mm(_sigmoid(_mm(r, g1_ref[...])), g2_ref[...])
        low = _mm(k, wa1_ref[dr])
        lw = -RW_DECAY_SCALE * _sigmoid(w0_ref[dr:dr + 1] + _mm(jnp.tanh(low[:, :LORA_PAD]), w2_ref[dr]))
        ag = _sigmoid(a0_ref[dr:dr + 1] + _mm(low[:, LORA_PAD:], a2_ref[dr]))
        kd = k * (1.0 + (ag - 1.0) * ka_ref[...])
        a = -kk
        b = kk * ag
        g = _seg_cumsum(lw, reverse)
        g_prev = g - lw
        g_end = _chunk_rows(g, 0 if reverse else CHUNK - 1)
        g_mid = _chunk_rows(g, CHUNK - 1 - CHUNK // 2 if reverse else CHUNK // 2)
        e_mid = jnp.exp(g_mid - g)
        e_end = jnp.exp(g_end - g)
        ops_ref[dr, OP_A_T] = _bf(a * jnp.exp(g_prev - g_mid))
        ops_ref[dr, OP_R_T] = _bf(r * jnp.exp(g - g_mid))
        ops_ref[dr, OP_B_T] = _bf(b * e_mid)
        ops_ref[dr, OP_K_T] = _bf(kd * e_mid)
        ops_ref[dr, OP_A_TRUE] = _bf(a * jnp.exp(g_prev))
        ops_ref[dr, OP_B_HAT] = _bf(b * e_end)
        ops_ref[dr, OP_K_HAT] = _bf(kd * e_end)
        ops_ref[dr, OP_V] = _bf(v)
        rtrue_ref[dr] = r * jnp.exp(g)
        gam_ref[dr] = jnp.exp(g_end)

    bm = ones_ref[0:RW_GROUP, 0:RW_GROUP]
    eye_cat = (_iota((CHUNK, RW_GROUP), 0) == (_iota((CHUNK, RW_GROUP), 1) & (CHUNK - 1))).astype(F32)

    def blk(x):
        return _tile4(x) * bm

    items = [(dr, grp, c) for dr in range(2) for grp in range(RW_GROUPS) for c in range(N_CHUNK)]

    def sl(c, grp):
        return slice(c * CHUNK, (c + 1) * CHUNK), slice(grp * RW_GROUP, (grp + 1) * RW_GROUP)

    for i, (dr, grp, c) in enumerate(items):
        rows, cols = sl(c, grp)
        lhs = jnp.concatenate([ops_ref[dr, OP_A_T, rows, cols], ops_ref[dr, OP_R_T, rows, cols]], axis=0)
        rhs = jnp.concatenate([blk(ops_ref[dr, OP_B_T, rows, cols]), blk(ops_ref[dr, OP_K_T, rows, cols])], axis=0)
        g = lax.dot_general(lhs, rhs, (((1,), (1,)), ((), ())), preferred_element_type=F32)
        strict = _causal_cat(dr == 1, strict=True)
        incl = _causal_cat(dr == 1, strict=False)
        g_ab = jnp.where(strict, g[:CHUNK, :RW_GROUP], 0.0)
        l_ref[i] = g_ab
        x_ref[i] = eye_cat + g_ab
        gak_ref[i] = _bf(jnp.where(strict, g[:CHUNK, RW_GROUP:], 0.0))
        grb_ref[i] = _bf(jnp.where(incl, g[CHUNK:, :RW_GROUP], 0.0))
        grk_ref[i] = _bf(jnp.where(incl, g[CHUNK:, RW_GROUP:], 0.0))

    n_sq = int(math.log2(CHUNK)) - 1
    for i in range(RW_ITEMS):
        lp = _bf(l_ref[i])
        l_ref[i] = _dot(lp, blk(lp))
    for step in range(1, n_sq):
        for i in range(RW_ITEMS):
            lp = _bf(l_ref[i])
            x = x_ref[i]
            res = _dot(jnp.concatenate([_bf(x), lp], axis=0), blk(lp))
            x_ref[i] = x + res[:CHUNK]
            l_ref[i] = res[CHUNK:]
    for i in range(RW_ITEMS):
        x = x_ref[i]
        x_ref[i] = x + _dot(_bf(x), blk(_bf(l_ref[i])))

    for i, (dr, grp, c) in enumerate(items):
        rows, cols = sl(c, grp)
        gv = _dot(jnp.concatenate([gak_ref[i], grk_ref[i]], axis=0), blk(ops_ref[dr, OP_V, rows, cols]))
        gak_ref[i] = _bf(gv[:CHUNK])
        yloc_ref[i] = gv[CHUNK:]

    for i, (dr, grp, c) in enumerate(items):
        rows, cols = sl(c, grp)
        rhs = jnp.concatenate([blk(ops_ref[dr, OP_A_TRUE, rows, cols]), blk(gak_ref[i])], axis=1)
        wu = _dot(_bf(x_ref[i]), rhs)
        rw_ref[i, CHUNK:, :] = _bf(wu[:, :RW_GROUP])
        uloc_ref[i] = wu[:, RW_GROUP:]

    for i, (dr, grp, c) in enumerate(items):
        rows, cols = sl(c, grp)
        rhs = jnp.concatenate([blk(rw_ref[i, CHUNK:, :]), blk(_bf(uloc_ref[i]))], axis=1)
        gwu = _dot(grb_ref[i], rhs)
        rw_ref[i, :CHUNK, :] = _bf(rtrue_ref[dr, rows, cols] + gwu[:, :RW_GROUP])
        yloc_ref[i] = gwu[:, RW_GROUP:] + yloc_ref[i]

    bm_f = bm.astype(F32)
    for step in range(N_CHUNK):
        for dr in range(2):
            c = N_CHUNK - 1 - step if dr == 1 else step
            y_ref = streams[dr][3]
            for grp in range(RW_GROUPS):
                i = (dr * RW_GROUPS + grp) * N_CHUNK + c
                rows, cols = sl(c, grp)
                zt = zt_ref[dr, grp]
                rw = lax.dot_general(rw_ref[i], _bf(zt), (((1,), (1,)), ((), ())), preferred_element_type=F32)
                y_ref[rows, cols] = rw[:CHUNK] + yloc_ref[i]
                u = rw[CHUNK:] + uloc_ref[i]
                uv = jnp.concatenate([_bf(u), ops_ref[dr, OP_V, rows, cols]], axis=0)
                bk = jnp.concatenate([ops_ref[dr, OP_B_HAT, rows, cols], ops_ref[dr, OP_K_HAT, rows, cols]], axis=0)
                upd = lax.dot_general(uv, bk, (((0,), (0,)), ((), ())), preferred_element_type=F32)
                zt_ref[dr, grp] = zt * gam_ref[dr, c * CHUNK:c * CHUNK + 1, cols] + upd * bm_f


def _rwkv(proj, n_tiles, p):
    rf, rb = _proj_specs(n_tiles, COL_RW_R)
    kf, kb = _proj_specs(n_tiles, COL_RW_K)
    vf, vb = _proj_specs(n_tiles, COL_RW_V)
    of, ob = _out_specs(n_tiles)
    n_tok = proj.shape[0]
    kkf, kkb = _proj_specs(n_tiles, COL_RW_KK)
    consts = (p['rw_k_a'], p['rw_r_k'], p['ones64'],
              p['rw_w0'], p['rw_wa1'], p['rw_w2'], p['rw_a0'], p['rw_a2'], p['rw_g1'], p['rw_g2'])
    out = jax.ShapeDtypeStruct((n_tok, BRANCH_W), F32)
    return pl.pallas_call(
        _rwkv_kernel,
        grid=(n_tiles,),
        in_specs=[rf, kf, vf, kkf, rb, kb, vb, kkb] + [_full(c) for c in consts],
        out_specs=[of, ob, of, of],
        out_shape=[out, out, out, out],
        scratch_shapes=[
            pltpu.VMEM((2, RW_GROUPS, RW_GROUP, RW_GROUP), F32),
            pltpu.VMEM((2, 8, TILE, BRANCH_W), BF16),
            pltpu.VMEM((2, TILE, BRANCH_W), F32),
            pltpu.VMEM((2, TILE, BRANCH_W), F32),
            pltpu.VMEM((RW_ITEMS, CHUNK, RW_GROUP), F32),
            pltpu.VMEM((RW_ITEMS, CHUNK, RW_GROUP), F32),
            pltpu.VMEM((RW_ITEMS, CHUNK, RW_GROUP), BF16),
            pltpu.VMEM((RW_ITEMS, CHUNK, RW_GROUP), BF16),
            pltpu.VMEM((RW_ITEMS, CHUNK, RW_GROUP), BF16),
            pltpu.VMEM((RW_ITEMS, 2 * CHUNK, RW_GROUP), BF16),
            pltpu.VMEM((RW_ITEMS, CHUNK, RW_GROUP), F32),
            pltpu.VMEM((RW_ITEMS, CHUNK, RW_GROUP), F32),
        ],
        compiler_params=_cparams(),
        name="rwkv_scan",
    )(proj, proj, proj, proj, proj, proj, proj, proj, *consts)


def _merge_kernel(x_ref, mod_ref, u_ref, gg_ref, lg_ref,
                  s5f_ref, s5b_ref, glf_ref, glb_ref, lrf_ref, lrb_ref, rwf_ref, rwb_ref, bonus_ref, rgate_ref,
                  s5d_ref, glu_ref, gnw_ref, ones64_ref, ones128_ref, lnw_ref, lnb_ref,
                  wg_ref, bg_ref, wb_ref, wo_ref, ln1g_ref, ln1b_ref, o_ref, *, dn_alpha):
    m = _mod_rows(mod_ref, pl.program_id(0) == 0)
    x = x_ref[...]
    h = _bf(x * (1.0 + _mod_part(m, 1)) + _mod_part(m, 0))

    ya = _gelu(s5f_ref[...] + s5b_ref[...] + s5d_ref[...] * u_ref[...])
    ya = ya * _sigmoid(_mm(ya, glu_ref[...]))

    o = glf_ref[...] + glb_ref[...]
    ms = _seg_sum(o * o, ones128_ref[...]) * (1.0 / GLA_DV)
    yb = (o * lax.rsqrt(ms + LN_EPS)) * gnw_ref[...] * _silu(gg_ref[...])

    yc = (lrf_ref[...] + lrb_ref[...]) * _gelu(lg_ref[...])

    y = rwf_ref[...] + rwb_ref[...]
    mu = _seg_sum(y, ones64_ref[...]) * (1.0 / RW_N)
    yc0 = y - mu
    var = _seg_sum(yc0 * yc0, ones64_ref[...]) * (1.0 / RW_N)
    yn = (yc0 * lax.rsqrt(var + RW_GN_EPS)) * lnw_ref[...] + lnb_ref[...]
    yd = (yn + bonus_ref[...]) * rgate_ref[...]

    z = jnp.zeros((TILE, D_MODEL), F32)
    for n, yn_ in enumerate((ya, yb, yc, yd)):
        cols = slice(n * D_MODEL, (n + 1) * D_MODEL)
        gate = _sigmoid(jnp.dot(h, wg_ref[:, cols], preferred_element_type=F32) + bg_ref[:, cols])
        z = z + gate * _mm(yn_, wb_ref[n])
    out = _mm(z, wo_ref[...])
    x1 = dn_alpha * x + _mod_part(m, 2) * out
    o_ref[...] = _layer_norm(x1, LN_EPS) * ln1g_ref[...] + ln1b_ref[...]


def _merge(xs, mod, proj, parts, n_tiles, p, dn_alpha):
    n_tok = xs.shape[0]
    tile_d = pl.BlockSpec((TILE, D_MODEL), lambda i: (i, 0))
    tile_w = pl.BlockSpec((TILE, BRANCH_W), lambda i: (i, 0))

    def pcol(col):
        return pl.BlockSpec((TILE, BRANCH_W), lambda i: (i, col))

    consts = (p['s5_d'], p['s5_w_glu'], p['gla_norm_w'], p['ones64'], p['ones128'], p['rw_ln_w'], p['rw_ln_b'],
              p['w_gate'], p['b_gate'], p['w_branch'], p['w_out'], p['ln1_g'], p['ln1_b'])
    return pl.pallas_call(
        functools.partial(_merge_kernel, dn_alpha=dn_alpha),
        grid=(n_tiles,),
        in_specs=[tile_d, _full(mod), pcol(COL_S5_U), pcol(COL_GLA_G), pcol(COL_LRU_GATE)]
                 + [tile_w] * len(parts) + [_full(c) for c in consts],
        out_specs=tile_d,
        out_shape=jax.ShapeDtypeStruct((n_tok, D_MODEL), F32),
        compiler_params=_cparams(),
        name="merge_ln1",
    )(xs, mod, proj, proj, proj, *parts, *consts)


def _ffn_kernel(x_ref, mod_ref, wup_ref, cw_ref, cb_ref, wdn_ref, ln2g_ref, ln2b_ref, o_ref, *,
                dn_alpha, first_tile):
    is_ctx = pl.program_id(0) + first_tile == 0
    m = _mod_rows(mod_ref, is_ctx)
    x = x_ref[...]
    h = _bf(x * (1.0 + _mod_part(m, 4)) + _mod_part(m, 3))
    acc = jnp.zeros((TILE, D_MODEL), F32)
    masks_by_width = {w: _shift_masks((TILE, w), (-1, 1), is_ctx) for w in set(FFN_CHUNKS)}
    start = 0
    for width in FFN_CHUNKS:
        ucols = slice(start, start + width)
        vcols = slice(FFN_HIDDEN + start, FFN_HIDDEN + start + width)
        start += width
        masks = masks_by_width[width]
        u = jnp.dot(h, wup_ref[:, ucols], preferred_element_type=F32)
        v = jnp.dot(h, wup_ref[:, vcols], preferred_element_type=F32)
        uc = (cb_ref[:, ucols] + cw_ref[1:2, ucols] * u
              + cw_ref[0:1, ucols] * _shift_rows(u, -1, masks) + cw_ref[2:3, ucols] * _shift_rows(u, 1, masks))
        acc = acc + _mm(_gelu(uc) * v, wdn_ref[ucols, :])
    x2 = dn_alpha * x + _mod_part(m, 5) * acc
    o_ref[...] = _layer_norm(x2, LN_EPS) * ln2g_ref[...] + ln2b_ref[...]


def _ffn(xs, mod, p, dn_alpha, skip_ctx):
    first = 1 if skip_ctx else 0
    n_out = xs.shape[0] - first * TILE
    consts = (p['ffn_w_up'], p['ffn_conv_w'], p['ffn_conv_b'], p['ffn_w_down'], p['ln2_g'], p['ln2_b'])
    return pl.pallas_call(
        functools.partial(_ffn_kernel, dn_alpha=dn_alpha, first_tile=first),
        grid=(n_out // TILE,),
        in_specs=[pl.BlockSpec((TILE, D_MODEL), lambda i: (i + first, 0)), _full(mod)] + [_full(c) for c in consts],
        out_specs=pl.BlockSpec((TILE, D_MODEL), lambda i: (i, 0)),
        out_shape=jax.ShapeDtypeStruct((n_out, D_MODEL), F32),
        compiler_params=_cparams(),
        name="convffn_ln2",
    )(xs, mod, *consts)


def _block_diag(blocks):
    n, a, b = blocks.shape
    eye = jnp.eye(n, dtype=blocks.dtype)
    return (eye[:, None, :, None] * blocks[:, :, None, :]).reshape(n * a, n * b)


def _pad_cols(w, n):
    return jnp.pad(w, [(0, 0)] * (w.ndim - 1) + [(0, n - w.shape[-1])])


def _pad_rows(w, n):
    return jnp.pad(w, [(0, 0)] * (w.ndim - 2) + [(0, n - w.shape[-2]), (0, 0)])


def _s5_perms():
    perm = np.zeros((2, TILE, TILE), np.float32)
    for j in range(S5_STEPS):
        for s in range(S5_SUB):
            t = s * S5_STEPS + j
            perm[0, j * S5_SUB + s, t] = 1.0
            perm[1, j * S5_SUB + s, TILE - 1 - t] = 1.0
    return jnp.asarray(perm, BF16), jnp.asarray(np.transpose(perm, (0, 2, 1)), BF16)


def _ones_blocks(width):
    idx = np.arange(BRANCH_W) // width
    return jnp.asarray(idx[:, None] == idx[None, :], BF16)


def _layer_params(q, i):
    p = {}
    row = lambda v: v[i].reshape(1, -1)
    dt = jnp.exp(q['s5_log_step'][i])[:, :, None]
    lam_re, lam_im = q['s5_lam_re'][i], q['s5_lam_im'][i]
    mag = jnp.exp(lam_re * dt)
    a_re, a_im = mag * jnp.cos(lam_im * dt), mag * jnp.sin(lam_im * dt)
    den = lam_re * lam_re + lam_im * lam_im
    f_re = ((a_re - 1.0) * lam_re + a_im * lam_im) / den
    f_im = (a_im * lam_re - (a_re - 1.0) * lam_im) / den
    b_re, b_im = q['s5_b_re'][i], q['s5_b_im'][i]
    d_re = f_re[..., None] * b_re[None] - f_im[..., None] * b_im[None]
    d_im = f_re[..., None] * b_im[None] + f_im[..., None] * b_re[None]
    gh = S5_G // 2
    halves = [slice(0, gh), slice(gh, S5_G)]
    bd = lambda blocks: _block_diag(jnp.swapaxes(blocks, 1, 2))
    p['s5_drive'] = jnp.stack([jnp.stack([jnp.concatenate([bd(d_re[dr, hs]), bd(d_im[dr, hs])], axis=1)
                                          for hs in halves]) for dr in range(2)]).astype(BF16)
    p['s5_a'] = jnp.concatenate([jnp.concatenate([a_re[:, hs].reshape(2, 1, -1), a_im[:, hs].reshape(2, 1, -1)],
                                                 axis=-1) for hs in halves], axis=-1)
    c_re, c_im = q['s5_c_re'][i], q['s5_c_im'][i]
    p['s5_read'] = jnp.stack([jnp.concatenate([bd(c_re[hs]), -bd(c_im[hs])], axis=0)
                              for hs in halves]).astype(BF16)
    p['s5_perm'], p['s5_perm_t'] = _s5_perms()
    p['s5_d'] = row(q['s5_d'])
    p['s5_w_glu'] = q['s5_w_glu'][i].astype(BF16)
    p['gla_w1'] = _pad_cols(q['gla_w1'][i], LORA_PAD).astype(BF16)
    p['gla_w2'] = _pad_rows(q['gla_w2'][i], LORA_PAD).astype(BF16)
    p['gla_bias'] = q['gla_bias'][i]
    p['gla_norm_w'] = jnp.tile(q['gla_norm_w'][i], GLA_HEADS).reshape(1, -1)
    p['lru_conv_w'] = q['lru_conv_w'][i]
    p['lru_conv_b'] = row(q['lru_conv_b'])
    p['lru_wa'] = jnp.stack([_block_diag(q['lru_w_a'][i, dr]) for dr in range(2)]).astype(BF16)
    p['lru_wx'] = jnp.stack([_block_diag(q['lru_w_x'][i, dr]) for dr in range(2)]).astype(BF16)
    p['lru_b_a'], p['lru_b_x'] = q['lru_b_a'][i], q['lru_b_x'][i]
    p['lru_c'] = -LRU_C * jax.nn.softplus(-q['lru_lam'][i])
    p['rw_mu'] = q['rw_mu'][i]
    for name in ('rw_k_k', 'rw_k_a', 'rw_r_k', 'rw_ln_w', 'rw_ln_b'):
        p[name] = row(q[name])
    p['rw_w0'], p['rw_a0'] = q['rw_w0'][i], q['rw_a0'][i]
    p['rw_wa1'] = jnp.concatenate([_pad_cols(q['rw_w1'][i], LORA_PAD), _pad_cols(q['rw_a1'][i], LORA_PAD)],
                                  axis=-1).astype(BF16)
    p['rw_w2'] = _pad_rows(q['rw_w2'][i], LORA_PAD).astype(BF16)
    p['rw_a2'] = _pad_rows(q['rw_a2'][i], LORA_PAD).astype(BF16)
    p['rw_g1'] = q['rw_g1'][i].astype(BF16)
    p['rw_g2'] = q['rw_g2'][i].astype(BF16)
    p['ones64'] = _ones_blocks(RW_N)
    p['ones128'] = _ones_blocks(GLA_DV)
    p['w_in'] = q['w_in'][i].astype(BF16)
    p['w_gate'] = q['w_gate'][i].astype(BF16)
    p['b_gate'] = row(q['b_gate'])
    p['w_branch'] = q['w_branch'][i].astype(BF16)
    p['w_out'] = q['w_out'][i].astype(BF16)
    p['ffn_w_up'] = q['ffn_w_up'][i].astype(BF16)
    p['ffn_w_down'] = q['ffn_w_down'][i].astype(BF16)
    p['ffn_conv_w'] = q['ffn_conv_w'][i]
    p['ffn_conv_b'] = row(q['ffn_conv_b'])
    for name in ('ln1_g', 'ln1_b', 'ln2_g', 'ln2_b'):
        p[name] = row(q[name])
    return p


def kernel(x, c, ctx, c_ctx, w_mod, b_mod, w_in, s5_lam_re, s5_lam_im, s5_log_step, s5_b_re, s5_b_im, s5_c_re, s5_c_im, s5_d, s5_w_glu, gla_w1, gla_w2, gla_bias, gla_norm_w, lru_conv_w, lru_conv_b, lru_w_a, lru_b_a, lru_w_x, lru_b_x, lru_lam, rw_mu, rw_w0, rw_w1, rw_w2, rw_a0, rw_a1, rw_a2, rw_g1, rw_g2, rw_k_k, rw_k_a, rw_r_k, rw_ln_w, rw_ln_b, w_branch, w_gate, b_gate, w_out, ln1_g, ln1_b, ln2_g, ln2_b, ffn_w_up, ffn_conv_w, ffn_conv_b, ffn_w_down):
    q = dict(w_in=w_in, s5_lam_re=s5_lam_re, s5_lam_im=s5_lam_im, s5_log_step=s5_log_step, s5_b_re=s5_b_re,
             s5_b_im=s5_b_im, s5_c_re=s5_c_re, s5_c_im=s5_c_im, s5_d=s5_d, s5_w_glu=s5_w_glu,
             gla_w1=gla_w1, gla_w2=gla_w2, gla_bias=gla_bias, gla_norm_w=gla_norm_w,
             lru_conv_w=lru_conv_w, lru_conv_b=lru_conv_b, lru_w_a=lru_w_a, lru_b_a=lru_b_a,
             lru_w_x=lru_w_x, lru_b_x=lru_b_x, lru_lam=lru_lam,
             rw_mu=rw_mu, rw_w0=rw_w0, rw_w1=rw_w1, rw_w2=rw_w2, rw_a0=rw_a0, rw_a1=rw_a1, rw_a2=rw_a2,
             rw_g1=rw_g1, rw_g2=rw_g2, rw_k_k=rw_k_k, rw_k_a=rw_k_a, rw_r_k=rw_r_k,
             rw_ln_w=rw_ln_w, rw_ln_b=rw_ln_b,
             w_branch=w_branch, w_gate=w_gate, b_gate=b_gate, w_out=w_out,
             ln1_g=ln1_g, ln1_b=ln1_b, ln2_g=ln2_g, ln2_b=ln2_b,
             ffn_w_up=ffn_w_up, ffn_conv_w=ffn_conv_w, ffn_conv_b=ffn_conv_b, ffn_w_down=ffn_w_down)
    batch, seq, d = x.shape
    assert batch == 1 and d == D_MODEL and ctx.shape[1] == CTX_LEN and seq % TILE == 0
    depth = w_mod.shape[0]
    dn_alpha = (2 * depth) ** 0.25
    n_tiles = 1 + seq // TILE

    cc = jnp.zeros((8, D_MODEL), F32).at[0].set(c[0]).at[1].set(c_ctx)
    mods = _modulation(cc, w_mod, b_mod)
    xs = _ln0(ctx[0], x[0])
    for i in range(depth):
        p = _layer_params(q, i)
        mod = mods[i]
        proj = _inproj(xs, mod, p)
        s5f, s5b = _s5(proj, n_tiles, p)
        glf, glb = _gla(proj, n_tiles, p)
        lrf, lrb = _lru(proj, n_tiles, p)
        rwf, rwb, bonus, rgate = _rwkv(proj, n_tiles, p)
        xs = _merge(xs, mod, proj, (s5f, s5b, glf, glb, lrf, lrb, rwf, rwb, bonus, rgate), n_tiles, p, dn_alpha)
        xs = _ffn(xs, mod, p, dn_alpha, skip_ctx=i == depth - 1)
    return xs.reshape(batch, seq, d)
```

```python
import functools
import math

import numpy as np
import jax
import jax.numpy as jnp
from jax import lax
from jax.experimental import pallas as pl
from jax.experimental.pallas import tpu as pltpu

F32 = jnp.float32
BF16 = jnp.bfloat16

D_MODEL = 1024
GRID_W = 64
CTX_LEN = 256
TILE = CTX_LEN
BRANCH_W = 512
N_BRANCH = 4

S5_H, S5_P = 16, 64
S5_G = BRANCH_W // S5_H
S5_STATE = S5_G * S5_P
S5_SUB = 8
S5_STEPS = TILE // S5_SUB
S5_UNROLL = True

GLA_HEADS, GLA_DK = 4, 64
GLA_DV = BRANCH_W // GLA_HEADS
GLA_QK = GLA_HEADS * GLA_DK
GLA_TAU = 16.0
CHUNK = 64
N_CHUNK = TILE // CHUNK

LRU_C = 8.0
LRU_CONV_LEFT = 2

RW_N = 64
RW_GROUP = 256
RW_GN_EPS = 64e-5
RW_DECAY_SCALE = math.exp(-0.5)

FFN_HIDDEN = 2816
FFN_CHUNKS = (1024, 1024, 768)
assert sum(FFN_CHUNKS) == FFN_HIDDEN
LORA_PAD = 128

LN_EPS = 1e-6
V7X_VMEM_LIMIT_BYTES = 56 * 1024 * 1024


def _bf(x):
    return x.astype(BF16)


def _mm(a, b):
    return jnp.dot(_bf(a), _bf(b), preferred_element_type=F32)


def _dot(a, b):
    return jnp.dot(a, b, preferred_element_type=F32)


def _mm_nt(a, b):
    return lax.dot_general(_bf(a), _bf(b), (((1,), (1,)), ((), ())), preferred_element_type=F32)


def _mm_tn(a, b):
    return lax.dot_general(_bf(a), _bf(b), (((0,), (0,)), ((), ())), preferred_element_type=F32)


def _split2(x):
    hi = _bf(x)
    lo = _bf(x - hi.astype(F32))
    return hi, lo


def _mm_hp(a, b):
    ah, al = _split2(a)
    bh, bl = _split2(b)
    dot = functools.partial(jnp.dot, preferred_element_type=F32)
    return dot(ah, bh) + (dot(ah, bl) + dot(al, bh))


def _seg_sum(x, ones_blk):
    return _dot(_bf(x), ones_blk)


def _iota(shape, dim):
    return lax.broadcasted_iota(jnp.int32, shape, dim)


def _sigmoid(x):
    return 0.5 * jnp.tanh(0.5 * x) + 0.5


def _silu(x):
    return x * _sigmoid(x)


def _gelu(x):
    return 0.5 * x * (1.0 + jnp.tanh(math.sqrt(2.0 / math.pi) * (x + 0.044715 * (x * x * x))))


def _log_sigmoid(x):
    return jnp.minimum(x, 0.0) - jnp.log1p(jnp.exp(-jnp.abs(x)))


def _neg_expm1_twice(y):
    t = jnp.tanh(y)
    return (-2.0 * t) / (1.0 - t)


def _layer_norm(x, eps):
    mu = jnp.mean(x, axis=-1, keepdims=True)
    xc = x - mu
    var = jnp.mean(xc * xc, axis=-1, keepdims=True)
    return xc * lax.rsqrt(var + eps)


def _shift_masks(shape, shifts, is_ctx):
    n = shape[0]
    t = _iota(shape, 0)
    pos = jnp.where(is_ctx, t, t & (GRID_W - 1))
    row_len = jnp.where(is_ctx, n, GRID_W)
    return {k: ((pos + k >= 0) & (pos + k < row_len)).astype(F32) for k in shifts}


def _shift_rows(x, k, masks):
    return pltpu.roll(x, (-k) % x.shape[0], axis=0) * masks[k]


def _seg_cumsum(x, reverse):
    n = x.shape[0]
    t = _iota(x.shape, 0) & (CHUNK - 1)
    d = 1
    while d < CHUNK:
        if reverse:
            x = x + jnp.where(t < CHUNK - d, pltpu.roll(x, n - d, axis=0), 0.0)
        else:
            x = x + jnp.where(t >= d, pltpu.roll(x, d, axis=0), 0.0)
        d *= 2
    return x


def _chunk_rows(x, row):
    parts = [jnp.broadcast_to(x[c * CHUNK + row:c * CHUNK + row + 1], (CHUNK, x.shape[1]))
             for c in range(x.shape[0] // CHUNK)]
    return jnp.concatenate(parts, axis=0)


def _tile4(x):
    return jnp.concatenate([x] * (RW_GROUP // CHUNK), axis=0)


def _block_mask(rows, cols, row_blk, col_blk):
    return (_iota((rows, cols), 0) // row_blk) == (_iota((rows, cols), 1) // col_blk)


def _causal_cat(reverse, strict):
    t = _iota((CHUNK, RW_GROUP), 0)
    s = _iota((CHUNK, RW_GROUP), 1) & (CHUNK - 1)
    if reverse:
        return (s > t) if strict else (s >= t)
    return (s < t) if strict else (s <= t)


def _bwd_tile(i, n_tiles):
    return jnp.where(i == 0, 0, n_tiles - i)


def _cparams():
    return pltpu.CompilerParams(dimension_semantics=("arbitrary",), vmem_limit_bytes=V7X_VMEM_LIMIT_BYTES)


def _full(arr):
    nd = arr.ndim
    return pl.BlockSpec(arr.shape, lambda i, _nd=nd: (0,) * _nd)


def _ln0_kernel(ctx_ref, x_ref, o_ref):
    is_ctx = pl.program_id(0) == 0
    o_ref[...] = _layer_norm(jnp.where(is_ctx, ctx_ref[...], x_ref[...]), LN_EPS)


def _ln0(ctx2d, x2d):
    n_tok = ctx2d.shape[0] + x2d.shape[0]
    return pl.pallas_call(
        _ln0_kernel,
        grid=(n_tok // TILE,),
        in_specs=[pl.BlockSpec((TILE, D_MODEL), lambda i: (0, 0)),
                  pl.BlockSpec((TILE, D_MODEL), lambda i: (jnp.maximum(i - 1, 0), 0))],
        out_specs=pl.BlockSpec((TILE, D_MODEL), lambda i: (i, 0)),
        out_shape=jax.ShapeDtypeStruct((n_tok, D_MODEL), F32),
        compiler_params=_cparams(),
        name="ln0",
    )(ctx2d, x2d)


MOD_COLS = 1536


def _mod_kernel(c_ref, w_ref, b_ref, o_ref):
    s = _silu(c_ref[...])
    o_ref[0] = _mm_hp(s, w_ref[0]) + b_ref[0]


def _modulation(cc, w_mod, b_mod):
    depth = w_mod.shape[0]
    n_col = w_mod.shape[2] // MOD_COLS
    return pl.pallas_call(
        _mod_kernel,
        grid=(depth, n_col),
        in_specs=[pl.BlockSpec((8, D_MODEL), lambda l, j: (0, 0)),
                  pl.BlockSpec((1, D_MODEL, MOD_COLS), lambda l, j: (l, 0, j)),
                  pl.BlockSpec((1, 1, MOD_COLS), lambda l, j: (l, 0, j))],
        out_specs=pl.BlockSpec((1, 8, MOD_COLS), lambda l, j: (l, 0, j)),
        out_shape=jax.ShapeDtypeStruct((depth, 8, w_mod.shape[2]), F32),
        compiler_params=pltpu.CompilerParams(dimension_semantics=("arbitrary", "arbitrary"),
                                             vmem_limit_bytes=V7X_VMEM_LIMIT_BYTES),
        name="modulation",
    )(cc, w_mod, b_mod.reshape(depth, 1, -1))


def _mod_rows(mod_ref, is_ctx):
    m = mod_ref[...]
    return jnp.where(is_ctx, m[1:2], m[0:1])


def _mod_part(m, j):
    return m[:, j * D_MODEL:(j + 1) * D_MODEL]


(COL_S5_U, COL_GLA_QK, COL_GLA_V, COL_GLA_G, COL_LRU_X, COL_LRU_GATE, COL_RW_R, COL_RW_K, COL_RW_V,
 COL_RW_KK) = range(10)
N_PROJ_COLS = 10 * BRANCH_W


def _col(block):
    return slice(block * BRANCH_W, (block + 1) * BRANCH_W)


def _inproj_kernel(x_ref, mod_ref, w_ref, cw_ref, cb_ref, mu_ref, kkw_ref, ones_ref, o_ref):
    is_ctx = pl.program_id(0) == 0
    m = _mod_rows(mod_ref, is_ctx)
    h = _bf(x_ref[...] * (1.0 + _mod_part(m, 1)) + _mod_part(m, 0))
    n_taps = cw_ref.shape[0]
    taps = [k - LRU_CONV_LEFT for k in range(n_taps)]
    masks = _shift_masks((TILE, BRANCH_W), sorted(set(taps + [-1, 1]) - {0}), is_ctx)

    x = _dot(h, w_ref[:, _col(COL_LRU_X)])
    o_ref[:, _col(COL_LRU_X)] = cb_ref[...] + sum(
        cw_ref[j:j + 1] * (_shift_rows(x, k, masks) if k else x) for j, k in enumerate(taps))

    for j, block in enumerate((COL_RW_R, COL_RW_K, COL_RW_V)):
        x = _dot(h, w_ref[:, _col(block)])
        xs = 0.5 * (_shift_rows(x, -1, masks) + _shift_rows(x, 1, masks))
        x = x + mu_ref[j:j + 1] * (xs - x)
        o_ref[:, _col(block)] = x
        if block == COL_RW_K:
            kk = x * kkw_ref[...]
            o_ref[:, _col(COL_RW_KK)] = kk / jnp.maximum(jnp.sqrt(_seg_sum(kk * kk, ones_ref[...])), 1e-12)

    plain = slice(0, COL_LRU_X * BRANCH_W)
    o_ref[:, plain] = _dot(h, w_ref[:, plain])
    o_ref[:, _col(COL_LRU_GATE)] = _dot(h, w_ref[:, _col(COL_LRU_GATE)])


def _inproj(xs, mod, p):
    n_tok = xs.shape[0]
    consts = (p['w_in'], p['lru_conv_w'], p['lru_conv_b'], p['rw_mu'], p['rw_k_k'], p['ones64'])
    return pl.pallas_call(
        _inproj_kernel,
        grid=(n_tok // TILE,),
        in_specs=[pl.BlockSpec((TILE, D_MODEL), lambda i: (i, 0)), _full(mod)] + [_full(c) for c in consts],
        out_specs=pl.BlockSpec((TILE, N_PROJ_COLS), lambda i: (i, 0)),
        out_shape=jax.ShapeDtypeStruct((n_tok, N_PROJ_COLS), F32),
        compiler_params=_cparams(),
        name="inproj",
    )(xs, mod, *consts)


def _proj_specs(n_tiles, col):
    fwd = pl.BlockSpec((TILE, BRANCH_W), lambda i: (i, col))
    bwd = pl.BlockSpec((TILE, BRANCH_W), lambda i: (_bwd_tile(i, n_tiles), col))
    return fwd, bwd


def _out_specs(n_tiles):
    fwd = pl.BlockSpec((TILE, BRANCH_W), lambda i: (i, 0))
    bwd = pl.BlockSpec((TILE, BRANCH_W), lambda i: (_bwd_tile(i, n_tiles), 0))
    return fwd, bwd


def _permute_rows(perm, x):
    hi, lo = _split2(x)
    return _dot(perm, hi) + _dot(perm, lo)


def _lru_kernel(xf_ref, xb_ref, perm_ref, permt_ref, wa_ref, ba_ref, wx_ref, bx_ref, lc_ref,
                hf_ref, hb_ref, carry_ref):
    @pl.when(pl.program_id(0) == 0)
    def _():
        carry_ref[...] = jnp.zeros_like(carry_ref)

    a, b = [], []
    for dr, x_ref in enumerate((xf_ref, xb_ref)):
        xc = _permute_rows(perm_ref[dr], x_ref[...])
        r = _sigmoid(_mm(xc, wa_ref[dr]) + ba_ref[dr:dr + 1])
        ig = _sigmoid(_mm(xc, wx_ref[dr]) + bx_ref[dr:dr + 1])
        log_a = lc_ref[dr:dr + 1] * r
        a.append(jnp.exp(log_a))
        b.append(jnp.sqrt(_neg_expm1_twice(log_a)) * (ig * xc))

    h = [jnp.zeros((S5_SUB, BRANCH_W), F32)] * 2
    prod = [jnp.ones((S5_SUB, BRANCH_W), F32)] * 2
    h_steps, prod_steps = [[], []], [[], []]
    for j in range(S5_STEPS):
        rows = slice(j * S5_SUB, (j + 1) * S5_SUB)
        for dr in range(2):
            h[dr] = a[dr][rows] * h[dr] + b[dr][rows]
            prod[dr] = prod[dr] * a[dr][rows]
            h_steps[dr].append(h[dr])
            prod_steps[dr].append(prod[dr])
    sub = _iota((S5_SUB, BRANCH_W), 0)
    for dr, o_ref in enumerate((hf_ref, hb_ref)):
        state = carry_ref[dr:dr + 1]
        entry = jnp.zeros((S5_SUB, BRANCH_W), F32)
        for s in range(S5_SUB):
            entry = jnp.where(sub == s, state, entry)
            state = prod[dr][s:s + 1] * state + h[dr][s:s + 1]
        carry_ref[dr:dr + 1] = state
        h_perm = jnp.concatenate([h_steps[dr][j] + prod_steps[dr][j] * entry for j in range(S5_STEPS)], axis=0)
        o_ref[...] = _permute_rows(permt_ref[dr], h_perm)


def _lru(proj, n_tiles, p):
    xf, xb = _proj_specs(n_tiles, COL_LRU_X)
    of, ob = _out_specs(n_tiles)
    n_tok = proj.shape[0]
    consts = (p['s5_perm'], p['s5_perm_t'], p['lru_wa'], p['lru_b_a'], p['lru_wx'], p['lru_b_x'], p['lru_c'])
    out = jax.ShapeDtypeStruct((n_tok, BRANCH_W), F32)
    return pl.pallas_call(
        _lru_kernel,
        grid=(n_tiles,),
        in_specs=[xf, xb] + [_full(c) for c in consts],
        out_specs=[of, ob],
        out_shape=[out, out],
        scratch_shapes=[pltpu.VMEM((8, BRANCH_W), F32)],
        compiler_params=_cparams(),
        name="lru_scan",
    )(proj, proj, *consts)


S5_HALF = S5_STATE // 2


def _cmul(ar, ai, br, bi):
    return ar * br - ai * bi, ar * bi + ai * br


def _s5_cols(half):
    base = 2 * S5_HALF * half
    return slice(base, base + S5_HALF), slice(base + S5_HALF, base + 2 * S5_HALF)


def _s5_kernel(uf_ref, ub_ref, perm_ref, permt_ref, a_ref, drive_ref, read_ref,
               yf_ref, yb_ref, h_ref, carry_ref, pw_ref, hb_ref):
    @pl.when(pl.program_id(0) == 0)
    def _():
        carry_ref[...] = jnp.zeros_like(carry_ref)
        for dr in range(2):
            for half in range(2):
                re, im = _s5_cols(half)
                a_re, a_im = a_ref[dr, :, re], a_ref[dr, :, im]
                q_re, q_im = a_re, a_im
                for j in range(0, S5_STEPS, 2):
                    n_re, n_im = _cmul(q_re, q_im, a_re, a_im)
                    rows = slice(j * S5_SUB, (j + 2) * S5_SUB)
                    pw_ref[dr, rows, re] = _bf(jnp.concatenate(
                        [jnp.broadcast_to(q_re, (S5_SUB, S5_HALF)), jnp.broadcast_to(n_re, (S5_SUB, S5_HALF))], axis=0))
                    pw_ref[dr, rows, im] = _bf(jnp.concatenate(
                        [jnp.broadcast_to(q_im, (S5_SUB, S5_HALF)), jnp.broadcast_to(n_im, (S5_SUB, S5_HALF))], axis=0))
                    q_re, q_im = _cmul(n_re, n_im, a_re, a_im)

    sub = _iota((S5_SUB, S5_HALF), 0)
    half_u = BRANCH_W // 2
    u_refs, o_refs = (uf_ref, ub_ref), (yf_ref, yb_ref)
    u_perm = [_bf(_dot(perm_ref[dr], _bf(u_refs[dr][...]))) for dr in range(2)]

    def state_cols(half):
        return slice(2 * S5_HALF * half, 2 * S5_HALF * (half + 1))

    def drive(dr, half):
        h_ref[dr, :, state_cols(half)] = _dot(u_perm[dr][:, half * half_u:(half + 1) * half_u], drive_ref[dr, half])

    def scan(dr, half):
        re, im = _s5_cols(half)
        a_re = a_ref[dr, :, re]
        a_im = a_ref[dr, :, im]
        a_re8 = jnp.broadcast_to(a_re, (S5_SUB, S5_HALF))
        a_im8 = jnp.broadcast_to(a_im, (S5_SUB, S5_HALF))
        h_re = jnp.zeros((S5_SUB, S5_HALF), F32)
        h_im = jnp.zeros((S5_SUB, S5_HALF), F32)
        for j in range(S5_STEPS):
            rows = slice(j * S5_SUB, (j + 1) * S5_SUB)
            m_re, m_im = _cmul(a_re8, a_im8, h_re, h_im)
            h_re = m_re + h_ref[dr, rows, re]
            h_im = m_im + h_ref[dr, rows, im]
            h_ref[dr, rows, re] = h_re
            h_ref[dr, rows, im] = h_im
        p_re, p_im = a_re, a_im
        for _ in range(int(math.log2(S5_STEPS))):
            p_re, p_im = _cmul(p_re, p_im, p_re, p_im)
        s_re = carry_ref[dr, :, re]
        s_im = carry_ref[dr, :, im]
        in_re = jnp.zeros((S5_SUB, S5_HALF), F32)
        in_im = jnp.zeros((S5_SUB, S5_HALF), F32)
        for s in range(S5_SUB):
            in_re = jnp.where(sub == s, s_re, in_re)
            in_im = jnp.where(sub == s, s_im, in_im)
            m_re, m_im = _cmul(p_re, p_im, s_re, s_im)
            s_re = m_re + h_re[s:s + 1]
            s_im = m_im + h_im[s:s + 1]
        carry_ref[dr, :, re] = s_re
        carry_ref[dr, :, im] = s_im
        in2_re = _bf(jnp.concatenate([in_re, in_re], axis=0))
        in2_im = _bf(jnp.concatenate([in_im, in_im], axis=0))
        for j in range(0, S5_STEPS, 2):
            rows = slice(j * S5_SUB, (j + 2) * S5_SUB)
            c_re, c_im = _cmul(pw_ref[dr, rows, re], pw_ref[dr, rows, im], in2_re, in2_im)
            hb_ref[dr, rows, re] = _bf(h_ref[dr, rows, re]) + c_re
            hb_ref[dr, rows, im] = _bf(h_ref[dr, rows, im]) + c_im

    def readout(dr):
        y_p = jnp.concatenate([_dot(hb_ref[dr, :, state_cols(half)], read_ref[half]) for half in range(2)], axis=1)
        hi, lo = _split2(y_p)
        o_refs[dr][...] = _dot(permt_ref[dr], hi) + _dot(permt_ref[dr], lo)

    for dr in range(2):
        for half in range(2):
            drive(dr, half)
    for dr in range(2):
        for half in range(2):
            scan(dr, half)
        readout(dr)


def _s5(proj, n_tiles, p):
    uf, ub = _proj_specs(n_tiles, COL_S5_U)
    of, ob = _out_specs(n_tiles)
    n_tok = proj.shape[0]
    consts = (p['s5_perm'], p['s5_perm_t'], p['s5_a'], p['s5_drive'], p['s5_read'])
    out = jax.ShapeDtypeStruct((n_tok, BRANCH_W), F32)
    return pl.pallas_call(
        _s5_kernel,
        grid=(n_tiles,),
        in_specs=[uf, ub] + [_full(c) for c in consts],
        out_specs=[of, ob],
        out_shape=[out, out],
        scratch_shapes=[pltpu.VMEM((2, TILE, 2 * S5_STATE), F32),
                        pltpu.VMEM((2, 1, 2 * S5_STATE), F32),
                        pltpu.VMEM((2, TILE, 2 * S5_STATE), BF16),
                        pltpu.VMEM((2, TILE, 2 * S5_STATE), BF16)],
        compiler_params=_cparams(),
        name="s5_scan",
    )(proj, proj, *consts)


def _gla_kernel(qkf_ref, vf_ref, qkb_ref, vb_ref, w1_ref, w2_ref, bias_ref, of_ref, ob_ref, st_ref):
    @pl.when(pl.program_id(0) == 0)
    def _():
        st_ref[...] = jnp.zeros_like(st_ref)

    bm_k = _bf(_block_mask(RW_GROUP, GLA_QK, CHUNK, GLA_DK).astype(F32))
    bm_v = _bf(_block_mask(RW_GROUP, BRANCH_W, CHUNK, GLA_DV).astype(F32))
    bm_s = _block_mask(BRANCH_W, GLA_QK, GLA_DV, GLA_DK)
    streams = ((qkf_ref, vf_ref, of_ref), (qkb_ref, vb_ref, ob_ref))
    prep = []
    for dr, (qk_ref, v_ref, _) in enumerate(streams):
        reverse = dr == 1
        q = qk_ref[:, :GLA_QK] * (GLA_DK ** -0.5)
        k = qk_ref[:, GLA_QK:]
        la = _log_sigmoid(_mm(_mm(k, w1_ref[dr]), w2_ref[dr]) + bias_ref[dr:dr + 1]) * (1.0 / GLA_TAU)
        b = _seg_cumsum(la, reverse)
        b_end = _chunk_rows(b, 0 if reverse else CHUNK - 1)
        b_mid = _chunk_rows(b, CHUNK - 1 - CHUNK // 2 if reverse else CHUNK // 2)
        prep.append(dict(qe=_bf(q * jnp.exp(b - b_mid)), ke=_bf(k * jnp.exp(b_mid - b)),
                         kl=_bf(k * jnp.exp(b_end - b)), qb=_bf(q * jnp.exp(b)), decay=jnp.exp(b_end),
                         v=_bf(v_ref[...]), causal=_causal_cat(reverse, strict=False)))
    states = [st_ref[0], st_ref[1]]
    for step in range(N_CHUNK):
        for dr, (_, _, o_ref) in enumerate(streams):
            c = N_CHUNK - 1 - step if dr == 1 else step
            rows = slice(c * CHUNK, (c + 1) * CHUNK)
            t = prep[dr]
            ke_blk = _tile4(t['ke'][rows]) * bm_k
            att = jnp.where(t['causal'], _mm_nt(t['qe'][rows], ke_blk), 0.0)
            v_blk = _tile4(t['v'][rows]) * bm_v
            o_ref[rows, :] = _mm(att, v_blk) + _mm_nt(t['qb'][rows], states[dr])
            states[dr] = (states[dr] * t['decay'][c * CHUNK:c * CHUNK + 1]
                          + jnp.where(bm_s, _mm_tn(t['v'][rows], t['kl'][rows]), 0.0))
    st_ref[0], st_ref[1] = states


def _gla(proj, n_tiles, p):
    qkf, qkb = _proj_specs(n_tiles, COL_GLA_QK)
    vf, vb = _proj_specs(n_tiles, COL_GLA_V)
    of, ob = _out_specs(n_tiles)
    n_tok = proj.shape[0]
    consts = (p['gla_w1'], p['gla_w2'], p['gla_bias'])
    out = jax.ShapeDtypeStruct((n_tok, BRANCH_W), F32)
    return pl.pallas_call(
        _gla_kernel,
        grid=(n_tiles,),
        in_specs=[qkf, vf, qkb, vb] + [_full(c) for c in consts],
        out_specs=[of, ob],
        out_shape=[out, out],
        scratch_shapes=[pltpu.VMEM((2, BRANCH_W, GLA_QK), F32)],
        compiler_params=_cparams(),
        name="gla_scan",
    )(proj, proj, proj, proj, *consts)


RW_GROUPS = BRANCH_W // RW_GROUP
RW_ITEMS = 2 * RW_GROUPS * N_CHUNK
OP_A_T, OP_R_T, OP_B_T, OP_K_T, OP_A_TRUE, OP_B_HAT, OP_K_HAT, OP_V = range(8)


def _rwkv_kernel(rf_ref, kf_ref, vf_ref, kkf_ref, rb_ref, kb_ref, vb_ref, kkb_ref,
                 ka_ref, rk_ref, ones_ref,
                 w0_ref, wa1_ref, w2_ref, a0_ref, a2_ref, g1_ref, g2_ref,
                 yf_ref, yb_ref, bonus_ref, gate_ref,
                 zt_ref, ops_ref, rtrue_ref, gam_ref, l_ref, x_ref, gak_ref, grb_ref, grk_ref,
                 rw_ref, yloc_ref, uloc_ref):
    is_ctx = pl.program_id(0) == 0

    @pl.when(is_ctx)
    def _():
        zt_ref[...] = jnp.zeros_like(zt_ref)

    streams = ((rf_ref, kf_ref, vf_ref, yf_ref), (rb_ref, kb_ref, vb_ref, yb_ref))
    for dr, (r_ref, k_ref, v_ref, _) in enumerate(streams):
        reverse = dr == 1
        r, k, v = r_ref[...], k_ref[...], v_ref[...]
        kk = (kkb_ref if reverse else kkf_ref)[...]
        if dr == 0:
            bonus_ref[...] = _seg_sum(r * k * rk_ref[...], ones_ref[...]) * v
            gate_ref[...] = _mm(_sigmoid(_mm(r, g1_ref[...])), g2_ref[...])
        low = _mm(k, wa1_ref[dr])
        lw = -RW_DECAY_SCALE * _sigmoid(w0_ref[dr:dr + 1] + _mm(jnp.tanh(low[:, :LORA_PAD]), w2_ref[dr]))
        ag = _sigmoid(a0_ref[dr:dr + 1] + _mm(low[:, LORA_PAD:], a2_ref[dr]))
        kd = k * (1.0 + (ag - 1.0) * ka_ref[...])
        a = -kk
        b = kk * ag
        g = _seg_cumsum(lw, reverse)
        g_prev = g - lw
        g_end = _chunk_rows(g, 0 if reverse else CHUNK - 1)
        g_mid = _chunk_rows(g, CHUNK - 1 - CHUNK // 2 if reverse else CHUNK // 2)
        e_mid = jnp.exp(g_mid - g)
        e_end = jnp.exp(g_end - g)
        ops_ref[dr, OP_A_T] = _bf(a * jnp.exp(g_prev - g_mid))
        ops_ref[dr, OP_R_T] = _bf(r * jnp.exp(g - g_mid))
        ops_ref[dr, OP_B_T] = _bf(b * e_mid)
        ops_ref[dr, OP_K_T] = _bf(kd * e_mid)
        ops_ref[dr, OP_A_TRUE] = _bf(a * jnp.exp(g_prev))
        ops_ref[dr, OP_B_HAT] = _bf(b * e_end)
        ops_ref[dr, OP_K_HAT] = _bf(kd * e_end)
        ops_ref[dr, OP_V] = _bf(v)
        rtrue_ref[dr] = r * jnp.exp(g)
        gam_ref[dr] = jnp.exp(g_end)

    bm = ones_ref[0:RW_GROUP, 0:RW_GROUP]
    eye_cat = (_iota((CHUNK, RW_GROUP), 0) == (_iota((CHUNK, RW_GROUP), 1) & (CHUNK - 1))).astype(F32)

    def blk(x):
        return _tile4(x) * bm

    items = [(dr, grp, c) for dr in range(2) for grp in range(RW_GROUPS) for c in range(N_CHUNK)]

    def sl(c, grp):
        return slice(c * CHUNK, (c + 1) * CHUNK), slice(grp * RW_GROUP, (grp + 1) * RW_GROUP)

    for i, (dr, grp, c) in enumerate(items):
        rows, cols = sl(c, grp)
        lhs = jnp.concatenate([ops_ref[dr, OP_A_T, rows, cols], ops_ref[dr, OP_R_T, rows, cols]], axis=0)
        rhs = jnp.concatenate([blk(ops_ref[dr, OP_B_T, rows, cols]), blk(ops_ref[dr, OP_K_T, rows, cols])], axis=0)
        g = lax.dot_general(lhs, rhs, (((1,), (1,)), ((), ())), preferred_element_type=F32)
        strict = _causal_cat(dr == 1, strict=True)
        incl = _causal_cat(dr == 1, strict=False)
        g_ab = jnp.where(strict, g[:CHUNK, :RW_GROUP], 0.0)
        l_ref[i] = g_ab
        x_ref[i] = eye_cat + g_ab
        gak_ref[i] = _bf(jnp.where(strict, g[:CHUNK, RW_GROUP:], 0.0))
        grb_ref[i] = _bf(jnp.where(incl, g[CHUNK:, :RW_GROUP], 0.0))
        grk_ref[i] = _bf(jnp.where(incl, g[CHUNK:, RW_GROUP:], 0.0))

    n_sq = int(math.log2(CHUNK)) - 1
    for i in range(RW_ITEMS):
        lp = _bf(l_ref[i])
        l_ref[i] = _dot(lp, blk(lp))
    for step in range(1, n_sq):
        for i in range(RW_ITEMS):
            lp = _bf(l_ref[i])
            x = x_ref[i]
            res = _dot(jnp.concatenate([_bf(x), lp], axis=0), blk(lp))
            x_ref[i] = x + res[:CHUNK]
            l_ref[i] = res[CHUNK:]
    for i in range(RW_ITEMS):
        x = x_ref[i]
        x_ref[i] = x + _dot(_bf(x), blk(_bf(l_ref[i])))

    for i, (dr, grp, c) in enumerate(items):
        rows, cols = sl(c, grp)
        gv = _dot(jnp.concatenate([gak_ref[i], grk_ref[i]], axis=0), blk(ops_ref[dr, OP_V, rows, cols]))
        gak_ref[i] = _bf(gv[:CHUNK])
        yloc_ref[i] = gv[CHUNK:]

    for i, (dr, grp, c) in enumerate(items):
        rows, cols = sl(c, grp)
        rhs = jnp.concatenate([blk(ops_ref[dr, OP_A_TRUE, rows, cols]), blk(gak_ref[i])], axis=1)
        wu = _dot(_bf(x_ref[i]), rhs)
        rw_ref[i, CHUNK:, :] = _bf(wu[:, :RW_GROUP])
        uloc_ref[i] = wu[:, RW_GROUP:]

    for i, (dr, grp, c) in enumerate(items):
        rows, cols = sl(c, grp)
        rhs = jnp.concatenate([blk(rw_ref[i, CHUNK:, :]), blk(_bf(uloc_ref[i]))], axis=1)
        gwu = _dot(grb_ref[i], rhs)
        rw_ref[i, :CHUNK, :] = _bf(rtrue_ref[dr, rows, cols] + gwu[:, :RW_GROUP])
        yloc_ref[i] = gwu[:, RW_GROUP:] + yloc_ref[i]

    bm_f = bm.astype(F32)
    for step in range(N_CHUNK):
        for dr in range(2):
            c = N_CHUNK - 1 - step if dr == 1 else step
            y_ref = streams[dr][3]
            for grp in range(RW_GROUPS):
                i = (dr * RW_GROUPS + grp) * N_CHUNK + c
                rows, cols = sl(c, grp)
                zt = zt_ref[dr, grp]
                rw = lax.dot_general(rw_ref[i], _bf(zt), (((1,), (1,)), ((), ())), preferred_element_type=F32)
                y_ref[rows, cols] = rw[:CHUNK] + yloc_ref[i]
                u = rw[CHUNK:] + uloc_ref[i]
                uv = jnp.concatenate([_bf(u), ops_ref[dr, OP_V, rows, cols]], axis=0)
                bk = jnp.concatenate([ops_ref[dr, OP_B_HAT, rows, cols], ops_ref[dr, OP_K_HAT, rows, cols]], axis=0)
                upd = lax.dot_general(uv, bk, (((0,), (0,)), ((), ())), preferred_element_type=F32)
                zt_ref[dr, grp] = zt * gam_ref[dr, c * CHUNK:c * CHUNK + 1, cols] + upd * bm_f


def _rwkv(proj, n_tiles, p):
    rf, rb = _proj_specs(n_tiles, COL_RW_R)
    kf, kb = _proj_specs(n_tiles, COL_RW_K)
    vf, vb = _proj_specs(n_tiles, COL_RW_V)
    of, ob = _out_specs(n_tiles)
    n_tok = proj.shape[0]
    kkf, kkb = _proj_specs(n_tiles, COL_RW_KK)
    consts = (p['rw_k_a'], p['rw_r_k'], p['ones64'],
              p['rw_w0'], p['rw_wa1'], p['rw_w2'], p['rw_a0'], p['rw_a2'], p['rw_g1'], p['rw_g2'])
    out = jax.ShapeDtypeStruct((n_tok, BRANCH_W), F32)
    return pl.pallas_call(
        _rwkv_kernel,
        grid=(n_tiles,),
        in_specs=[rf, kf, vf, kkf, rb, kb, vb, kkb] + [_full(c) for c in consts],
        out_specs=[of, ob, of, of],
        out_shape=[out, out, out, out],
        scratch_shapes=[
            pltpu.VMEM((2, RW_GROUPS, RW_GROUP, RW_GROUP), F32),
            pltpu.VMEM((2, 8, TILE, BRANCH_W), BF16),
            pltpu.VMEM((2, TILE, BRANCH_W), F32),
            pltpu.VMEM((2, TILE, BRANCH_W), F32),
            pltpu.VMEM((RW_ITEMS, CHUNK, RW_GROUP), F32),
            pltpu.VMEM((RW_ITEMS, CHUNK, RW_GROUP), F32),
            pltpu.VMEM((RW_ITEMS, CHUNK, RW_GROUP), BF16),
            pltpu.VMEM((RW_ITEMS, CHUNK, RW_GROUP), BF16),
            pltpu.VMEM((RW_ITEMS, CHUNK, RW_GROUP), BF16),
            pltpu.VMEM((RW_ITEMS, 2 * CHUNK, RW_GROUP), BF16),
            pltpu.VMEM((RW_ITEMS, CHUNK, RW_GROUP), F32),
            pltpu.VMEM((RW_ITEMS, CHUNK, RW_GROUP), F32),
        ],
        compiler_params=_cparams(),
        name="rwkv_scan",
    )(proj, proj, proj, proj, proj, proj, proj, proj, *consts)


def _merge_kernel(x_ref, mod_ref, u_ref, gg_ref, lg_ref,
                  s5f_ref, s5b_ref, glf_ref, glb_ref, lrf_ref, lrb_ref, rwf_ref, rwb_ref, bonus_ref, rgate_ref,
                  s5d_ref, glu_ref, gnw_ref, ones64_ref, ones128_ref, lnw_ref, lnb_ref,
                  wg_ref, bg_ref, wb_ref, wo_ref, ln1g_ref, ln1b_ref, o_ref, *, dn_alpha):
    m = _mod_rows(mod_ref, pl.program_id(0) == 0)
    x = x_ref[...]
    h = _bf(x * (1.0 + _mod_part(m, 1)) + _mod_part(m, 0))

    def s5_readout():
        ya = _gelu(s5f_ref[...] + s5b_ref[...] + s5d_ref[...] * u_ref[...])
        return ya * _sigmoid(_mm(ya, glu_ref[...]))

    def gla_readout():
        o = glf_ref[...] + glb_ref[...]
        ms = _seg_sum(o * o, ones128_ref[...]) * (1.0 / GLA_DV)
        return (o * lax.rsqrt(ms + LN_EPS)) * gnw_ref[...] * _silu(gg_ref[...])

    def lru_readout():
        return (lrf_ref[...] + lrb_ref[...]) * _gelu(lg_ref[...])

    def rwkv_readout():
        y = rwf_ref[...] + rwb_ref[...]
        mu = _seg_sum(y, ones64_ref[...]) * (1.0 / RW_N)
        yc0 = y - mu
        var = _seg_sum(yc0 * yc0, ones64_ref[...]) * (1.0 / RW_N)
        yn = (yc0 * lax.rsqrt(var + RW_GN_EPS)) * lnw_ref[...] + lnb_ref[...]
        return (yn + bonus_ref[...]) * rgate_ref[...]

    ys = (s5_readout(), gla_readout(), lru_readout(), rwkv_readout())
    z = jnp.zeros((TILE, D_MODEL), F32)
    for n, y_n in enumerate(ys):
        cols = slice(n * D_MODEL, (n + 1) * D_MODEL)
        gate = _sigmoid(jnp.dot(h, wg_ref[:, cols], preferred_element_type=F32) + bg_ref[:, cols])
        z = z + gate * _mm(y_n, wb_ref[n])
    out = _mm(z, wo_ref[...])
    x1 = dn_alpha * x + _mod_part(m, 2) * out
    o_ref[...] = _layer_norm(x1, LN_EPS) * ln1g_ref[...] + ln1b_ref[...]


def _merge(xs, mod, proj, parts, n_tiles, p, dn_alpha):
    n_tok = xs.shape[0]
    tile_d = pl.BlockSpec((TILE, D_MODEL), lambda i: (i, 0))
    tile_w = pl.BlockSpec((TILE, BRANCH_W), lambda i: (i, 0))

    def pcol(col):
        return pl.BlockSpec((TILE, BRANCH_W), lambda i: (i, col))

    consts = (p['s5_d'], p['s5_w_glu'], p['gla_norm_w'], p['ones64'], p['ones128'], p['rw_ln_w'], p['rw_ln_b'],
              p['w_gate'], p['b_gate'], p['w_branch'], p['w_out'], p['ln1_g'], p['ln1_b'])
    return pl.pallas_call(
        functools.partial(_merge_kernel, dn_alpha=dn_alpha),
        grid=(n_tiles,),
        in_specs=[tile_d, _full(mod), pcol(COL_S5_U), pcol(COL_GLA_G), pcol(COL_LRU_GATE)]
                 + [tile_w] * len(parts) + [_full(c) for c in consts],
        out_specs=tile_d,
        out_shape=jax.ShapeDtypeStruct((n_tok, D_MODEL), F32),
        compiler_params=_cparams(),
        name="merge_ln1",
    )(xs, mod, proj, proj, proj, *parts, *consts)


def _ffn_kernel(x_ref, mod_ref, wup_ref, cw_ref, cb_ref, wdn_ref, ln2g_ref, ln2b_ref, o_ref, *,
                dn_alpha, first_tile):
    is_ctx = pl.program_id(0) + first_tile == 0
    m = _mod_rows(mod_ref, is_ctx)
    x = x_ref[...]
    h = _bf(x * (1.0 + _mod_part(m, 4)) + _mod_part(m, 3))
    acc = jnp.zeros((TILE, D_MODEL), F32)
    masks_by_width = {w: _shift_masks((TILE, w), (-1, 1), is_ctx) for w in set(FFN_CHUNKS)}
    start = 0
    for width in FFN_CHUNKS:
        ucols = slice(start, start + width)
        vcols = slice(FFN_HIDDEN + start, FFN_HIDDEN + start + width)
        start += width
        masks = masks_by_width[width]
        u = jnp.dot(h, wup_ref[:, ucols], preferred_element_type=F32)
        v = jnp.dot(h, wup_ref[:, vcols], preferred_element_type=F32)
        uc = (cb_ref[:, ucols] + cw_ref[1:2, ucols] * u
              + cw_ref[0:1, ucols] * _shift_rows(u, -1, masks) + cw_ref[2:3, ucols] * _shift_rows(u, 1, masks))
        acc = acc + _mm(_gelu(uc) * v, wdn_ref[ucols, :])
    x2 = dn_alpha * x + _mod_part(m, 5) * acc
    o_ref[...] = _layer_norm(x2, LN_EPS) * ln2g_ref[...] + ln2b_ref[...]


def _ffn(xs, mod, p, dn_alpha, skip_ctx):
    first = 1 if skip_ctx else 0
    n_out = xs.shape[0] - first * TILE
    consts = (p['ffn_w_up'], p['ffn_conv_w'], p['ffn_conv_b'], p['ffn_w_down'], p['ln2_g'], p['ln2_b'])
    return pl.pallas_call(
        functools.partial(_ffn_kernel, dn_alpha=dn_alpha, first_tile=first),
        grid=(n_out // TILE,),
        in_specs=[pl.BlockSpec((TILE, D_MODEL), lambda i: (i + first, 0)), _full(mod)] + [_full(c) for c in consts],
        out_specs=pl.BlockSpec((TILE, D_MODEL), lambda i: (i, 0)),
        out_shape=jax.ShapeDtypeStruct((n_out, D_MODEL), F32),
        compiler_params=_cparams(),
        name="convffn_ln2",
    )(xs, mod, *consts)


def _block_diag(blocks):
    n, a, b = blocks.shape
    eye = jnp.eye(n, dtype=blocks.dtype)
    return (eye[:, None, :, None] * blocks[:, :, None, :]).reshape(n * a, n * b)


def _pad_cols(w, n):
    return jnp.pad(w, [(0, 0)] * (w.ndim - 1) + [(0, n - w.shape[-1])])


def _pad_rows(w, n):
    return jnp.pad(w, [(0, 0)] * (w.ndim - 2) + [(0, n - w.shape[-2]), (0, 0)])


def _s5_perms():
    perm = np.zeros((2, TILE, TILE), np.float32)
    for j in range(S5_STEPS):
        for s in range(S5_SUB):
            t = s * S5_STEPS + j
            perm[0, j * S5_SUB + s, t] = 1.0
            perm[1, j * S5_SUB + s, TILE - 1 - t] = 1.0
    return jnp.asarray(perm, BF16), jnp.asarray(np.transpose(perm, (0, 2, 1)), BF16)


def _ones_blocks(width):
    idx = np.arange(BRANCH_W) // width
    return jnp.asarray(idx[:, None] == idx[None, :], BF16)


def _layer_params(q, i):
    p = {}
    row = lambda v: v[i].reshape(1, -1)
    dt = jnp.exp(q['s5_log_step'][i])[:, :, None]
    lam_re, lam_im = q['s5_lam_re'][i], q['s5_lam_im'][i]
    mag = jnp.exp(lam_re * dt)
    a_re, a_im = mag * jnp.cos(lam_im * dt), mag * jnp.sin(lam_im * dt)
    den = lam_re * lam_re + lam_im * lam_im
    f_re = ((a_re - 1.0) * lam_re + a_im * lam_im) / den
    f_im = (a_im * lam_re - (a_re - 1.0) * lam_im) / den
    b_re, b_im = q['s5_b_re'][i], q['s5_b_im'][i]
    d_re = f_re[..., None] * b_re[None] - f_im[..., None] * b_im[None]
    d_im = f_re[..., None] * b_im[None] + f_im[..., None] * b_re[None]
    gh = S5_G // 2
    halves = [slice(0, gh), slice(gh, S5_G)]
    bd = lambda blocks: _block_diag(jnp.swapaxes(blocks, 1, 2))
    p['s5_drive'] = jnp.stack([jnp.stack([jnp.concatenate([bd(d_re[dr, hs]), bd(d_im[dr, hs])], axis=1)
                                          for hs in halves]) for dr in range(2)]).astype(BF16)
    p['s5_a'] = jnp.concatenate([jnp.concatenate([a_re[:, hs].reshape(2, 1, -1), a_im[:, hs].reshape(2, 1, -1)],
                                                 axis=-1) for hs in halves], axis=-1)
    c_re, c_im = q['s5_c_re'][i], q['s5_c_im'][i]
    p['s5_read'] = jnp.stack([jnp.concatenate([bd(c_re[hs]), -bd(c_im[hs])], axis=0)
                              for hs in halves]).astype(BF16)
    p['s5_perm'], p['s5_perm_t'] = _s5_perms()
    p['s5_d'] = row(q['s5_d'])
    p['s5_w_glu'] = q['s5_w_glu'][i].astype(BF16)
    p['gla_w1'] = _pad_cols(q['gla_w1'][i], LORA_PAD).astype(BF16)
    p['gla_w2'] = _pad_rows(q['gla_w2'][i], LORA_PAD).astype(BF16)
    p['gla_bias'] = q['gla_bias'][i]
    p['gla_norm_w'] = jnp.tile(q['gla_norm_w'][i], GLA_HEADS).reshape(1, -1)
    p['lru_conv_w'] = q['lru_conv_w'][i]
    p['lru_conv_b'] = row(q['lru_conv_b'])
    p['lru_wa'] = jnp.stack([_block_diag(q['lru_w_a'][i, dr]) for dr in range(2)]).astype(BF16)
    p['lru_wx'] = jnp.stack([_block_diag(q['lru_w_x'][i, dr]) for dr in range(2)]).astype(BF16)
    p['lru_b_a'], p['lru_b_x'] = q['lru_b_a'][i], q['lru_b_x'][i]
    p['lru_c'] = -LRU_C * jax.nn.softplus(-q['lru_lam'][i])
    p['rw_mu'] = q['rw_mu'][i]
    for name in ('rw_k_k', 'rw_k_a', 'rw_r_k', 'rw_ln_w', 'rw_ln_b'):
        p[name] = row(q[name])
    p['rw_w0'], p['rw_a0'] = q['rw_w0'][i], q['rw_a0'][i]
    p['rw_wa1'] = jnp.concatenate([_pad_cols(q['rw_w1'][i], LORA_PAD), _pad_cols(q['rw_a1'][i], LORA_PAD)],
                                  axis=-1).astype(BF16)
    p['rw_w2'] = _pad_rows(q['rw_w2'][i], LORA_PAD).astype(BF16)
    p['rw_a2'] = _pad_rows(q['rw_a2'][i], LORA_PAD).astype(BF16)
    p['rw_g1'] = q['rw_g1'][i].astype(BF16)
    p['rw_g2'] = q['rw_g2'][i].astype(BF16)
    p['ones64'] = _ones_blocks(RW_N)
    p['ones128'] = _ones_blocks(GLA_DV)
    p['w_in'] = q['w_in'][i].astype(BF16)
    p['w_gate'] = q['w_gate'][i].astype(BF16)
    p['b_gate'] = row(q['b_gate'])
    p['w_branch'] = q['w_branch'][i].astype(BF16)
    p['w_out'] = q['w_out'][i].astype(BF16)
    p['ffn_w_up'] = q['ffn_w_up'][i].astype(BF16)
    p['ffn_w_down'] = q['ffn_w_down'][i].astype(BF16)
    p['ffn_conv_w'] = q['ffn_conv_w'][i]
    p['ffn_conv_b'] = row(q['ffn_conv_b'])
    for name in ('ln1_g', 'ln1_b', 'ln2_g', 'ln2_b'):
        p[name] = row(q[name])
    return p


def kernel(x, c, ctx, c_ctx, w_mod, b_mod, w_in, s5_lam_re, s5_lam_im, s5_log_step, s5_b_re, s5_b_im, s5_c_re, s5_c_im, s5_d, s5_w_glu, gla_w1, gla_w2, gla_bias, gla_norm_w, lru_conv_w, lru_conv_b, lru_w_a, lru_b_a, lru_w_x, lru_b_x, lru_lam, rw_mu, rw_w0, rw_w1, rw_w2, rw_a0, rw_a1, rw_a2, rw_g1, rw_g2, rw_k_k, rw_k_a, rw_r_k, rw_ln_w, rw_ln_b, w_branch, w_gate, b_gate, w_out, ln1_g, ln1_b, ln2_g, ln2_b, ffn_w_up, ffn_conv_w, ffn_conv_b, ffn_w_down):
    q = dict(w_in=w_in, s5_lam_re=s5_lam_re, s5_lam_im=s5_lam_im, s5_log_step=s5_log_step, s5_b_re=s5_b_re,
             s5_b_im=s5_b_im, s5_c_re=s5_c_re, s5_c_im=s5_c_im, s5_d=s5_d, s5_w_glu=s5_w_glu,
             gla_w1=gla_w1, gla_w2=gla_w2, gla_bias=gla_bias, gla_norm_w=gla_norm_w,
             lru_conv_w=lru_conv_w, lru_conv_b=lru_conv_b, lru_w_a=lru_w_a, lru_b_a=lru_b_a,
             lru_w_x=lru_w_x, lru_b_x=lru_b_x, lru_lam=lru_lam,
             rw_mu=rw_mu, rw_w0=rw_w0, rw_w1=rw_w1, rw_w2=rw_w2, rw_a0=rw_a0, rw_a1=rw_a1, rw_a2=rw_a2,
             rw_g1=rw_g1, rw_g2=rw_g2, rw_k_k=rw_k_k, rw_k_a=rw_k_a, rw_r_k=rw_r_k,
             rw_ln_w=rw_ln_w, rw_ln_b=rw_ln_b,
             w_branch=w_branch, w_gate=w_gate, b_gate=b_gate, w_out=w_out,
             ln1_g=ln1_g, ln1_b=ln1_b, ln2_g=ln2_g, ln2_b=ln2_b,
             ffn_w_up=ffn_w_up, ffn_conv_w=ffn_conv_w, ffn_conv_b=ffn_conv_b, ffn_w_down=ffn_w_down)
    batch, seq, d = x.shape
    assert batch == 1 and d == D_MODEL and ctx.shape[1] == CTX_LEN and seq % TILE == 0
    depth = w_mod.shape[0]
    dn_alpha = (2 * depth) ** 0.25
    n_tiles = 1 + seq // TILE

    cc = jnp.zeros((8, D_MODEL), F32).at[0].set(c[0]).at[1].set(c_ctx)
    mods = _modulation(cc, w_mod, b_mod)
    xs = _ln0(ctx[0], x[0])
    for i in range(depth):
        p = _layer_params(q, i)
        mod = mods[i]
        proj = _inproj(xs, mod, p)
        s5f, s5b = _s5(proj, n_tiles, p)
        glf, glb = _gla(proj, n_tiles, p)
        lrf, lrb = _lru(proj, n_tiles, p)
        rwf, rwb, bonus, rgate = _rwkv(proj, n_tiles, p)
        xs = _merge(xs, mod, proj, (s5f, s5b, glf, glb, lrf, lrb, rwf, rwb, bonus, rgate), n_tiles, p, dn_alpha)
        xs = _ffn(xs, mod, p, dn_alpha, skip_ctx=i == depth - 1)
    return xs.reshape(batch, seq, d)
```

```python
import functools
import math

import numpy as np
import jax
import jax.numpy as jnp
from jax import lax
from jax.experimental import pallas as pl
from jax.experimental.pallas import tpu as pltpu

F32 = jnp.float32
BF16 = jnp.bfloat16

D_MODEL = 1024
GRID_W = 64
CTX_LEN = 256
TILE = CTX_LEN
BRANCH_W = 512
N_BRANCH = 4

S5_H, S5_P = 16, 64
S5_G = BRANCH_W // S5_H
S5_STATE = S5_G * S5_P
S5_SUB = 8
S5_STEPS = TILE // S5_SUB
S5_UNROLL = True

GLA_HEADS, GLA_DK = 4, 64
GLA_DV = BRANCH_W // GLA_HEADS
GLA_QK = GLA_HEADS * GLA_DK
GLA_TAU = 16.0
CHUNK = 64
N_CHUNK = TILE // CHUNK

LRU_C = 8.0
LRU_CONV_LEFT = 2

RW_N = 64
RW_GROUP = 256
RW_GN_EPS = 64e-5
RW_DECAY_SCALE = math.exp(-0.5)

FFN_HIDDEN = 2816
FFN_CHUNKS = (1024, 1024, 768)
assert sum(FFN_CHUNKS) == FFN_HIDDEN
LORA_PAD = 128

LN_EPS = 1e-6
V7X_VMEM_LIMIT_BYTES = 56 * 1024 * 1024


def _bf(x):
    return x.astype(BF16)


def _mm(a, b):
    return jnp.dot(_bf(a), _bf(b), preferred_element_type=F32)


def _dot(a, b):
    return jnp.dot(a, b, preferred_element_type=F32)


def _mm_nt(a, b):
    return lax.dot_general(_bf(a), _bf(b), (((1,), (1,)), ((), ())), preferred_element_type=F32)


def _mm_tn(a, b):
    return lax.dot_general(_bf(a), _bf(b), (((0,), (0,)), ((), ())), preferred_element_type=F32)


def _split2(x):
    hi = _bf(x)
    lo = _bf(x - hi.astype(F32))
    return hi, lo


def _mm_hp(a, b):
    ah, al = _split2(a)
    bh, bl = _split2(b)
    dot = functools.partial(jnp.dot, preferred_element_type=F32)
    return dot(ah, bh) + (dot(ah, bl) + dot(al, bh))


def _seg_sum(x, ones_blk):
    return _dot(_bf(x), ones_blk)


def _iota(shape, dim):
    return lax.broadcasted_iota(jnp.int32, shape, dim)


def _sigmoid(x):
    return 0.5 * jnp.tanh(0.5 * x) + 0.5


def _silu(x):
    return x * _sigmoid(x)


def _gelu(x):
    return 0.5 * x * (1.0 + jnp.tanh(math.sqrt(2.0 / math.pi) * (x + 0.044715 * (x * x * x))))


def _log_sigmoid(x):
    return jnp.minimum(x, 0.0) - jnp.log1p(jnp.exp(-jnp.abs(x)))


def _neg_expm1_twice(y):
    t = jnp.tanh(y)
    return (-2.0 * t) / (1.0 - t)


def _layer_norm(x, eps):
    mu = jnp.mean(x, axis=-1, keepdims=True)
    xc = x - mu
    var = jnp.mean(xc * xc, axis=-1, keepdims=True)
    return xc * lax.rsqrt(var + eps)


def _shift_masks(shape, shifts, is_ctx):
    n = shape[0]
    t = _iota(shape, 0)
    pos = jnp.where(is_ctx, t, t & (GRID_W - 1))
    row_len = jnp.where(is_ctx, n, GRID_W)
    return {k: ((pos + k >= 0) & (pos + k < row_len)).astype(F32) for k in shifts}


def _shift_rows(x, k, masks):
    return pltpu.roll(x, (-k) % x.shape[0], axis=0) * masks[k]


def _seg_cumsum(x, reverse):
    n = x.shape[0]
    t = _iota(x.shape, 0) & (CHUNK - 1)
    d = 1
    while d < CHUNK:
        if reverse:
            x = x + jnp.where(t < CHUNK - d, pltpu.roll(x, n - d, axis=0), 0.0)
        else:
            x = x + jnp.where(t >= d, pltpu.roll(x, d, axis=0), 0.0)
        d *= 2
    return x


def _chunk_rows(x, row):
    parts = [jnp.broadcast_to(x[c * CHUNK + row:c * CHUNK + row + 1], (CHUNK, x.shape[1]))
             for c in range(x.shape[0] // CHUNK)]
    return jnp.concatenate(parts, axis=0)


def _tile4(x):
    return jnp.concatenate([x] * (RW_GROUP // CHUNK), axis=0)


def _block_mask(rows, cols, row_blk, col_blk):
    return (_iota((rows, cols), 0) // row_blk) == (_iota((rows, cols), 1) // col_blk)


def _causal_cat(reverse, strict):
    t = _iota((CHUNK, RW_GROUP), 0)
    s = _iota((CHUNK, RW_GROUP), 1) & (CHUNK - 1)
    if reverse:
        return (s > t) if strict else (s >= t)
    return (s < t) if strict else (s <= t)


def _bwd_tile(i, n_tiles):
    return jnp.where(i == 0, 0, n_tiles - i)


def _cparams():
    return pltpu.CompilerParams(dimension_semantics=("arbitrary",), vmem_limit_bytes=V7X_VMEM_LIMIT_BYTES)


def _full(arr):
    nd = arr.ndim
    return pl.BlockSpec(arr.shape, lambda i, _nd=nd: (0,) * _nd)


class _LayerOf:
    def __init__(self, stacked, layer):
        self.stacked, self.layer = stacked, layer


def _const_spec(c):
    if isinstance(c, _LayerOf):
        nd = c.stacked.ndim
        return pl.BlockSpec((None,) + c.stacked.shape[1:], lambda i, _l=c.layer, _nd=nd: (_l,) + (0,) * (_nd - 1))
    return _full(c)


def _const_array(c):
    return c.stacked if isinstance(c, _LayerOf) else c


def _ln0_kernel(ctx_ref, x_ref, o_ref):
    is_ctx = pl.program_id(0) == 0
    o_ref[...] = _layer_norm(jnp.where(is_ctx, ctx_ref[...], x_ref[...]), LN_EPS)


def _ln0(ctx2d, x2d):
    n_tok = ctx2d.shape[0] + x2d.shape[0]
    return pl.pallas_call(
        _ln0_kernel,
        grid=(n_tok // TILE,),
        in_specs=[pl.BlockSpec((TILE, D_MODEL), lambda i: (0, 0)),
                  pl.BlockSpec((TILE, D_MODEL), lambda i: (jnp.maximum(i - 1, 0), 0))],
        out_specs=pl.BlockSpec((TILE, D_MODEL), lambda i: (i, 0)),
        out_shape=jax.ShapeDtypeStruct((n_tok, D_MODEL), F32),
        compiler_params=_cparams(),
        name="ln0",
    )(ctx2d, x2d)


MOD_COLS = 1536


def _mod_kernel(c_ref, w_ref, b_ref, o_ref):
    s = _silu(c_ref[...])
    o_ref[0] = _mm_hp(s, w_ref[0]) + b_ref[0]


def _modulation(cc, w_mod, b_mod):
    depth = w_mod.shape[0]
    n_col = w_mod.shape[2] // MOD_COLS
    return pl.pallas_call(
        _mod_kernel,
        grid=(depth, n_col),
        in_specs=[pl.BlockSpec((8, D_MODEL), lambda l, j: (0, 0)),
                  pl.BlockSpec((1, D_MODEL, MOD_COLS), lambda l, j: (l, 0, j)),
                  pl.BlockSpec((1, 1, MOD_COLS), lambda l, j: (l, 0, j))],
        out_specs=pl.BlockSpec((1, 8, MOD_COLS), lambda l, j: (l, 0, j)),
        out_shape=jax.ShapeDtypeStruct((depth, 8, w_mod.shape[2]), F32),
        compiler_params=pltpu.CompilerParams(dimension_semantics=("arbitrary", "arbitrary"),
                                             vmem_limit_bytes=V7X_VMEM_LIMIT_BYTES),
        name="modulation",
    )(cc, w_mod, b_mod.reshape(depth, 1, -1))


def _mod_rows(mod_ref, is_ctx):
    m = mod_ref[...]
    return jnp.where(is_ctx, m[1:2], m[0:1])


def _mod_part(m, j):
    return m[:, j * D_MODEL:(j + 1) * D_MODEL]


(COL_S5_U, COL_GLA_QK, COL_GLA_V, COL_GLA_G, COL_LRU_X, COL_LRU_GATE, COL_RW_R, COL_RW_K, COL_RW_V,
 COL_RW_KK) = range(10)
N_PROJ_COLS = 10 * BRANCH_W


def _col(block):
    return slice(block * BRANCH_W, (block + 1) * BRANCH_W)


def _inproj_kernel(x_ref, mod_ref, w_ref, cw_ref, cb_ref, mu_ref, kkw_ref, ones_ref, o_ref):
    is_ctx = pl.program_id(0) == 0
    m = _mod_rows(mod_ref, is_ctx)
    h = _bf(x_ref[...] * (1.0 + _mod_part(m, 1)) + _mod_part(m, 0))
    n_taps = cw_ref.shape[0]
    taps = [k - LRU_CONV_LEFT for k in range(n_taps)]
    masks = _shift_masks((TILE, BRANCH_W), sorted(set(taps + [-1, 1]) - {0}), is_ctx)

    x = _dot(h, w_ref[:, _col(COL_LRU_X)])
    o_ref[:, _col(COL_LRU_X)] = cb_ref[...] + sum(
        cw_ref[j:j + 1] * (_shift_rows(x, k, masks) if k else x) for j, k in enumerate(taps))

    for j, block in enumerate((COL_RW_R, COL_RW_K, COL_RW_V)):
        x = _dot(h, w_ref[:, _col(block)])
        xs = 0.5 * (_shift_rows(x, -1, masks) + _shift_rows(x, 1, masks))
        x = x + mu_ref[j:j + 1] * (xs - x)
        o_ref[:, _col(block)] = x
        if block == COL_RW_K:
            kk = x * kkw_ref[...]
            o_ref[:, _col(COL_RW_KK)] = kk / jnp.maximum(jnp.sqrt(_seg_sum(kk * kk, ones_ref[...])), 1e-12)

    plain = slice(0, COL_LRU_X * BRANCH_W)
    o_ref[:, plain] = _dot(h, w_ref[:, plain])
    o_ref[:, _col(COL_LRU_GATE)] = _dot(h, w_ref[:, _col(COL_LRU_GATE)])


def _inproj(xs, mod, p):
    n_tok = xs.shape[0]
    consts = (p['w_in'], p['lru_conv_w'], p['lru_conv_b'], p['rw_mu'], p['rw_k_k'], p['ones64'])
    return pl.pallas_call(
        _inproj_kernel,
        grid=(n_tok // TILE,),
        in_specs=[pl.BlockSpec((TILE, D_MODEL), lambda i: (i, 0)), _full(mod)] + [_const_spec(c) for c in consts],
        out_specs=pl.BlockSpec((TILE, N_PROJ_COLS), lambda i: (i, 0)),
        out_shape=jax.ShapeDtypeStruct((n_tok, N_PROJ_COLS), F32),
        compiler_params=_cparams(),
        name="inproj",
    )(xs, mod, *[_const_array(c) for c in consts])


def _proj_specs(n_tiles, col):
    fwd = pl.BlockSpec((TILE, BRANCH_W), lambda i: (i, col))
    bwd = pl.BlockSpec((TILE, BRANCH_W), lambda i: (_bwd_tile(i, n_tiles), col))
    return fwd, bwd


def _out_specs(n_tiles):
    fwd = pl.BlockSpec((TILE, BRANCH_W), lambda i: (i, 0))
    bwd = pl.BlockSpec((TILE, BRANCH_W), lambda i: (_bwd_tile(i, n_tiles), 0))
    return fwd, bwd


def _permute_rows(perm, x):
    hi, lo = _split2(x)
    return _dot(perm, hi) + _dot(perm, lo)


def _lru_kernel(xf_ref, xb_ref, perm_ref, permt_ref, wa_ref, ba_ref, wx_ref, bx_ref, lc_ref,
                hf_ref, hb_ref, carry_ref):
    @pl.when(pl.program_id(0) == 0)
    def _():
        carry_ref[...] = jnp.zeros_like(carry_ref)

    a, b = [], []
    for dr, x_ref in enumerate((xf_ref, xb_ref)):
        xc = _permute_rows(perm_ref[dr], x_ref[...])
        r = _sigmoid(_mm(xc, wa_ref[dr]) + ba_ref[dr:dr + 1])
        ig = _sigmoid(_mm(xc, wx_ref[dr]) + bx_ref[dr:dr + 1])
        log_a = lc_ref[dr:dr + 1] * r
        a.append(jnp.exp(log_a))
        b.append(jnp.sqrt(_neg_expm1_twice(log_a)) * (ig * xc))

    h = [jnp.zeros((S5_SUB, BRANCH_W), F32)] * 2
    prod = [jnp.ones((S5_SUB, BRANCH_W), F32)] * 2
    h_steps, prod_steps = [[], []], [[], []]
    for j in range(S5_STEPS):
        rows = slice(j * S5_SUB, (j + 1) * S5_SUB)
        for dr in range(2):
            h[dr] = a[dr][rows] * h[dr] + b[dr][rows]
            prod[dr] = prod[dr] * a[dr][rows]
            h_steps[dr].append(h[dr])
            prod_steps[dr].append(prod[dr])
    sub = _iota((S5_SUB, BRANCH_W), 0)
    for dr, o_ref in enumerate((hf_ref, hb_ref)):
        state = carry_ref[dr:dr + 1]
        entry = jnp.zeros((S5_SUB, BRANCH_W), F32)
        for s in range(S5_SUB):
            entry = jnp.where(sub == s, state, entry)
            state = prod[dr][s:s + 1] * state + h[dr][s:s + 1]
        carry_ref[dr:dr + 1] = state
        h_perm = jnp.concatenate([h_steps[dr][j] + prod_steps[dr][j] * entry for j in range(S5_STEPS)], axis=0)
        o_ref[...] = _permute_rows(permt_ref[dr], h_perm)


def _lru(proj, n_tiles, p):
    xf, xb = _proj_specs(n_tiles, COL_LRU_X)
    of, ob = _out_specs(n_tiles)
    n_tok = proj.shape[0]
    consts = (p['s5_perm'], p['s5_perm_t'], p['lru_wa'], p['lru_b_a'], p['lru_wx'], p['lru_b_x'], p['lru_c'])
    out = jax.ShapeDtypeStruct((n_tok, BRANCH_W), F32)
    return pl.pallas_call(
        _lru_kernel,
        grid=(n_tiles,),
        in_specs=[xf, xb] + [_const_spec(c) for c in consts],
        out_specs=[of, ob],
        out_shape=[out, out],
        scratch_shapes=[pltpu.VMEM((8, BRANCH_W), F32)],
        compiler_params=_cparams(),
        name="lru_scan",
    )(proj, proj, *[_const_array(c) for c in consts])


S5_HALF = S5_STATE // 2


def _cmul(ar, ai, br, bi):
    return ar * br - ai * bi, ar * bi + ai * br


def _s5_cols(half):
    base = 2 * S5_HALF * half
    return slice(base, base + S5_HALF), slice(base + S5_HALF, base + 2 * S5_HALF)


def _s5_kernel(uf_ref, ub_ref, perm_ref, permt_ref, a_ref, drive_ref, read_ref,
               yf_ref, yb_ref, h_ref, carry_ref, pw_ref, hb_ref):
    @pl.when(pl.program_id(0) == 0)
    def _():
        carry_ref[...] = jnp.zeros_like(carry_ref)
        for dr in range(2):
            for half in range(2):
                re, im = _s5_cols(half)
                a_re, a_im = a_ref[dr, :, re], a_ref[dr, :, im]
                q_re, q_im = a_re, a_im
                for j in range(0, S5_STEPS, 2):
                    n_re, n_im = _cmul(q_re, q_im, a_re, a_im)
                    rows = slice(j * S5_SUB, (j + 2) * S5_SUB)
                    pw_ref[dr, rows, re] = _bf(jnp.concatenate(
                        [jnp.broadcast_to(q_re, (S5_SUB, S5_HALF)), jnp.broadcast_to(n_re, (S5_SUB, S5_HALF))], axis=0))
                    pw_ref[dr, rows, im] = _bf(jnp.concatenate(
                        [jnp.broadcast_to(q_im, (S5_SUB, S5_HALF)), jnp.broadcast_to(n_im, (S5_SUB, S5_HALF))], axis=0))
                    q_re, q_im = _cmul(n_re, n_im, a_re, a_im)

    sub = _iota((S5_SUB, S5_HALF), 0)
    half_u = BRANCH_W // 2
    u_refs, o_refs = (uf_ref, ub_ref), (yf_ref, yb_ref)
    u_perm = [_bf(_dot(perm_ref[dr], _bf(u_refs[dr][...]))) for dr in range(2)]

    def state_cols(half):
        return slice(2 * S5_HALF * half, 2 * S5_HALF * (half + 1))

    def drive(dr, half):
        h_ref[dr, :, state_cols(half)] = _dot(u_perm[dr][:, half * half_u:(half + 1) * half_u], drive_ref[dr, half])

    def scan(dr, half):
        re, im = _s5_cols(half)
        a_re = a_ref[dr, :, re]
        a_im = a_ref[dr, :, im]
        a_re8 = jnp.broadcast_to(a_re, (S5_SUB, S5_HALF))
        a_im8 = jnp.broadcast_to(a_im, (S5_SUB, S5_HALF))
        h_re = jnp.zeros((S5_SUB, S5_HALF), F32)
        h_im = jnp.zeros((S5_SUB, S5_HALF), F32)
        for j in range(S5_STEPS):
            rows = slice(j * S5_SUB, (j + 1) * S5_SUB)
            m_re, m_im = _cmul(a_re8, a_im8, h_re, h_im)
            h_re = m_re + h_ref[dr, rows, re]
            h_im = m_im + h_ref[dr, rows, im]
            h_ref[dr, rows, re] = h_re
            h_ref[dr, rows, im] = h_im
        p_re, p_im = a_re, a_im
        for _ in range(int(math.log2(S5_STEPS))):
            p_re, p_im = _cmul(p_re, p_im, p_re, p_im)
        s_re = carry_ref[dr, :, re]
        s_im = carry_ref[dr, :, im]
        in_re = jnp.zeros((S5_SUB, S5_HALF), F32)
        in_im = jnp.zeros((S5_SUB, S5_HALF), F32)
        for s in range(S5_SUB):
            in_re = jnp.where(sub == s, s_re, in_re)
            in_im = jnp.where(sub == s, s_im, in_im)
            m_re, m_im = _cmul(p_re, p_im, s_re, s_im)
            s_re = m_re + h_re[s:s + 1]
            s_im = m_im + h_im[s:s + 1]
        carry_ref[dr, :, re] = s_re
        carry_ref[dr, :, im] = s_im
        in2_re = _bf(jnp.concatenate([in_re, in_re], axis=0))
        in2_im = _bf(jnp.concatenate([in_im, in_im], axis=0))
        for j in range(0, S5_STEPS, 2):
            rows = slice(j * S5_SUB, (j + 2) * S5_SUB)
            c_re, c_im = _cmul(pw_ref[dr, rows, re], pw_ref[dr, rows, im], in2_re, in2_im)
            hb_ref[dr, rows, re] = _bf(h_ref[dr, rows, re]) + c_re
            hb_ref[dr, rows, im] = _bf(h_ref[dr, rows, im]) + c_im

    def readout(dr):
        y_p = jnp.concatenate([_dot(hb_ref[dr, :, state_cols(half)], read_ref[half]) for half in range(2)], axis=1)
        hi, lo = _split2(y_p)
        o_refs[dr][...] = _dot(permt_ref[dr], hi) + _dot(permt_ref[dr], lo)

    for dr in range(2):
        for half in range(2):
            drive(dr, half)
    for dr in range(2):
        for half in range(2):
            scan(dr, half)
        readout(dr)


def _s5(proj, n_tiles, p):
    uf, ub = _proj_specs(n_tiles, COL_S5_U)
    of, ob = _out_specs(n_tiles)
    n_tok = proj.shape[0]
    consts = (p['s5_perm'], p['s5_perm_t'], p['s5_a'], p['s5_drive'], p['s5_read'])
    out = jax.ShapeDtypeStruct((n_tok, BRANCH_W), F32)
    return pl.pallas_call(
        _s5_kernel,
        grid=(n_tiles,),
        in_specs=[uf, ub] + [_const_spec(c) for c in consts],
        out_specs=[of, ob],
        out_shape=[out, out],
        scratch_shapes=[pltpu.VMEM((2, TILE, 2 * S5_STATE), F32),
                        pltpu.VMEM((2, 1, 2 * S5_STATE), F32),
                        pltpu.VMEM((2, TILE, 2 * S5_STATE), BF16),
                        pltpu.VMEM((2, TILE, 2 * S5_STATE), BF16)],
        compiler_params=_cparams(),
        name="s5_scan",
    )(proj, proj, *[_const_array(c) for c in consts])


def _gla_kernel(qkf_ref, vf_ref, qkb_ref, vb_ref, w1_ref, w2_ref, bias_ref, of_ref, ob_ref, st_ref):
    @pl.when(pl.program_id(0) == 0)
    def _():
        st_ref[...] = jnp.zeros_like(st_ref)

    bm_k = _bf(_block_mask(RW_GROUP, GLA_QK, CHUNK, GLA_DK).astype(F32))
    bm_v = _bf(_block_mask(RW_GROUP, BRANCH_W, CHUNK, GLA_DV).astype(F32))
    bm_s = _block_mask(BRANCH_W, GLA_QK, GLA_DV, GLA_DK)
    streams = ((qkf_ref, vf_ref, of_ref), (qkb_ref, vb_ref, ob_ref))
    prep = []
    for dr, (qk_ref, v_ref, _) in enumerate(streams):
        reverse = dr == 1
        q = qk_ref[:, :GLA_QK] * (GLA_DK ** -0.5)
        k = qk_ref[:, GLA_QK:]
        la = _log_sigmoid(_mm(_mm(k, w1_ref[dr]), w2_ref[dr]) + bias_ref[dr:dr + 1]) * (1.0 / GLA_TAU)
        b = _seg_cumsum(la, reverse)
        b_end = _chunk_rows(b, 0 if reverse else CHUNK - 1)
        b_mid = _chunk_rows(b, CHUNK - 1 - CHUNK // 2 if reverse else CHUNK // 2)
        prep.append(dict(qe=_bf(q * jnp.exp(b - b_mid)), ke=_bf(k * jnp.exp(b_mid - b)),
                         kl=_bf(k * jnp.exp(b_end - b)), qb=_bf(q * jnp.exp(b)), decay=jnp.exp(b_end),
                         v=_bf(v_ref[...]), causal=_causal_cat(reverse, strict=False)))
    states = [st_ref[0], st_ref[1]]
    for step in range(N_CHUNK):
        for dr, (_, _, o_ref) in enumerate(streams):
            c = N_CHUNK - 1 - step if dr == 1 else step
            rows = slice(c * CHUNK, (c + 1) * CHUNK)
            t = prep[dr]
            ke_blk = _tile4(t['ke'][rows]) * bm_k
            att = jnp.where(t['causal'], _mm_nt(t['qe'][rows], ke_blk), 0.0)
            v_blk = _tile4(t['v'][rows]) * bm_v
            o_ref[rows, :] = _mm(att, v_blk) + _mm_nt(t['qb'][rows], states[dr])
            states[dr] = (states[dr] * t['decay'][c * CHUNK:c * CHUNK + 1]
                          + jnp.where(bm_s, _mm_tn(t['v'][rows], t['kl'][rows]), 0.0))
    st_ref[0], st_ref[1] = states


def _gla(proj, n_tiles, p):
    qkf, qkb = _proj_specs(n_tiles, COL_GLA_QK)
    vf, vb = _proj_specs(n_tiles, COL_GLA_V)
    of, ob = _out_specs(n_tiles)
    n_tok = proj.shape[0]
    consts = (p['gla_w1'], p['gla_w2'], p['gla_bias'])
    out = jax.ShapeDtypeStruct((n_tok, BRANCH_W), F32)
    return pl.pallas_call(
        _gla_kernel,
        grid=(n_tiles,),
        in_specs=[qkf, vf, qkb, vb] + [_const_spec(c) for c in consts],
        out_specs=[of, ob],
        out_shape=[out, out],
        scratch_shapes=[pltpu.VMEM((2, BRANCH_W, GLA_QK), F32)],
        compiler_params=_cparams(),
        name="gla_scan",
    )(proj, proj, proj, proj, *[_const_array(c) for c in consts])


RW_GROUPS = BRANCH_W // RW_GROUP
RW_ITEMS = 2 * RW_GROUPS * N_CHUNK
OP_A_T, OP_R_T, OP_B_T, OP_K_T, OP_A_TRUE, OP_B_HAT, OP_K_HAT, OP_V = range(8)


def _rwkv_kernel(rf_ref, kf_ref, vf_ref, kkf_ref, rb_ref, kb_ref, vb_ref, kkb_ref,
                 ka_ref, rk_ref, ones_ref,
                 w0_ref, wa1_ref, w2_ref, a0_ref, a2_ref, g1_ref, g2_ref,
                 yf_ref, yb_ref, bonus_ref, gate_ref,
                 zt_ref, ops_ref, rtrue_ref, gam_ref, l_ref, x_ref, gak_ref, grb_ref, grk_ref,
                 rw_ref, yloc_ref, uloc_ref):
    is_ctx = pl.program_id(0) == 0

    @pl.when(is_ctx)
    def _():
        zt_ref[...] = jnp.zeros_like(zt_ref)

    streams = ((rf_ref, kf_ref, vf_ref, yf_ref), (rb_ref, kb_ref, vb_ref, yb_ref))
    for dr, (r_ref, k_ref, v_ref, _) in enumerate(streams):
        reverse = dr == 1
        r, k, v = r_ref[...], k_ref[...], v_ref[...]
        kk = (kkb_ref if reverse else kkf_ref)[...]
        if dr == 0:
            bonus_ref[...] = _seg_sum(r * k * rk_ref[...], ones_ref[...]) * v
            gate_ref[...] = _mm(_sigmoid(_mm(r, g1_ref[...])), g2_ref[...])
        low = _mm(k, wa1_ref[dr])
        lw = -RW_DECAY_SCALE * _sigmoid(w0_ref[dr:dr + 1] + _mm(jnp.tanh(low[:, :LORA_PAD]), w2_ref[dr]))
        ag = _sigmoid(a0_ref[dr:dr + 1] + _mm(low[:, LORA_PAD:], a2_ref[dr]))
        kd = k * (1.0 + (ag - 1.0) * ka_ref[...])
        a = -kk
        b = kk * ag
        g = _seg_cumsum(lw, reverse)
        g_prev = g - lw
        g_end = _chunk_rows(g, 0 if reverse else CHUNK - 1)
        g_mid = _chunk_rows(g, CHUNK - 1 - CHUNK // 2 if reverse else CHUNK // 2)
        e_mid = jnp.exp(g_mid - g)
        e_end = jnp.exp(g_end - g)
        ops_ref[dr, OP_A_T] = _bf(a * jnp.exp(g_prev - g_mid))
        ops_ref[dr, OP_R_T] = _bf(r * jnp.exp(g - g_mid))
        ops_ref[dr, OP_B_T] = _bf(b * e_mid)
        ops_ref[dr, OP_K_T] = _bf(kd * e_mid)
        ops_ref[dr, OP_A_TRUE] = _bf(a * jnp.exp(g_prev))
        ops_ref[dr, OP_B_HAT] = _bf(b * e_end)
        ops_ref[dr, OP_K_HAT] = _bf(kd * e_end)
        ops_ref[dr, OP_V] = _bf(v)
        rtrue_ref[dr] = r * jnp.exp(g)
        gam_ref[dr] = jnp.exp(g_end)

    bm = ones_ref[0:RW_GROUP, 0:RW_GROUP]
    eye_cat = (_iota((CHUNK, RW_GROUP), 0) == (_iota((CHUNK, RW_GROUP), 1) & (CHUNK - 1))).astype(F32)

    def blk(x):
        return _tile4(x) * bm

    items = [(dr, grp, c) for dr in range(2) for grp in range(RW_GROUPS) for c in range(N_CHUNK)]

    def sl(c, grp):
        return slice(c * CHUNK, (c + 1) * CHUNK), slice(grp * RW_GROUP, (grp + 1) * RW_GROUP)

    for i, (dr, grp, c) in enumerate(items):
        rows, cols = sl(c, grp)
        lhs = jnp.concatenate([ops_ref[dr, OP_A_T, rows, cols], ops_ref[dr, OP_R_T, rows, cols]], axis=0)
        rhs = jnp.concatenate([blk(ops_ref[dr, OP_B_T, rows, cols]), blk(ops_ref[dr, OP_K_T, rows, cols])], axis=0)
        g = lax.dot_general(lhs, rhs, (((1,), (1,)), ((), ())), preferred_element_type=F32)
        strict = _causal_cat(dr == 1, strict=True)
        incl = _causal_cat(dr == 1, strict=False)
        g_ab = jnp.where(strict, g[:CHUNK, :RW_GROUP], 0.0)
        l_ref[i] = g_ab
        x_ref[i] = eye_cat + g_ab
        gak_ref[i] = _bf(jnp.where(strict, g[:CHUNK, RW_GROUP:], 0.0))
        grb_ref[i] = _bf(jnp.where(incl, g[CHUNK:, :RW_GROUP], 0.0))
        grk_ref[i] = _bf(jnp.where(incl, g[CHUNK:, RW_GROUP:], 0.0))

    n_sq = int(math.log2(CHUNK)) - 1
    for i in range(RW_ITEMS):
        lp = _bf(l_ref[i])
        l_ref[i] = _dot(lp, blk(lp))
    for step in range(1, n_sq):
        for i in range(RW_ITEMS):
            lp = _bf(l_ref[i])
            x = x_ref[i]
            res = _dot(jnp.concatenate([_bf(x), lp], axis=0), blk(lp))
            x_ref[i] = x + res[:CHUNK]
            l_ref[i] = res[CHUNK:]
    for i in range(RW_ITEMS):
        x = x_ref[i]
        x_ref[i] = x + _dot(_bf(x), blk(_bf(l_ref[i])))

    for i, (dr, grp, c) in enumerate(items):
        rows, cols = sl(c, grp)
        gv = _dot(jnp.concatenate([gak_ref[i], grk_ref[i]], axis=0), blk(ops_ref[dr, OP_V, rows, cols]))
        gak_ref[i] = _bf(gv[:CHUNK])
        yloc_ref[i] = gv[CHUNK:]

    for i, (dr, grp, c) in enumerate(items):
        rows, cols = sl(c, grp)
        rhs = jnp.concatenate([blk(ops_ref[dr, OP_A_TRUE, rows, cols]), blk(gak_ref[i])], axis=1)
        wu = _dot(_bf(x_ref[i]), rhs)
        rw_ref[i, CHUNK:, :] = _bf(wu[:, :RW_GROUP])
        uloc_ref[i] = wu[:, RW_GROUP:]

    for i, (dr, grp, c) in enumerate(items):
        rows, cols = sl(c, grp)
        rhs = jnp.concatenate([blk(rw_ref[i, CHUNK:, :]), blk(_bf(uloc_ref[i]))], axis=1)
        gwu = _dot(grb_ref[i], rhs)
        rw_ref[i, :CHUNK, :] = _bf(rtrue_ref[dr, rows, cols] + gwu[:, :RW_GROUP])
        yloc_ref[i] = gwu[:, RW_GROUP:] + yloc_ref[i]

    bm_f = bm.astype(F32)
    for step in range(N_CHUNK):
        for dr in range(2):
            c = N_CHUNK - 1 - step if dr == 1 else step
            y_ref = streams[dr][3]
            for grp in range(RW_GROUPS):
                i = (dr * RW_GROUPS + grp) * N_CHUNK + c
                rows, cols = sl(c, grp)
                zt = zt_ref[dr, grp]
                rw = lax.dot_general(rw_ref[i], _bf(zt), (((1,), (1,)), ((), ())), preferred_element_type=F32)
                y_ref[rows, cols] = rw[:CHUNK] + yloc_ref[i]
                u = rw[CHUNK:] + uloc_ref[i]
                uv = jnp.concatenate([_bf(u), ops_ref[dr, OP_V, rows, cols]], axis=0)
                bk = jnp.concatenate([ops_ref[dr, OP_B_HAT, rows, cols], ops_ref[dr, OP_K_HAT, rows, cols]], axis=0)
                upd = lax.dot_general(uv, bk, (((0,), (0,)), ((), ())), preferred_element_type=F32)
                zt_ref[dr, grp] = zt * gam_ref[dr, c * CHUNK:c * CHUNK + 1, cols] + upd * bm_f


def _rwkv(proj, n_tiles, p):
    rf, rb = _proj_specs(n_tiles, COL_RW_R)
    kf, kb = _proj_specs(n_tiles, COL_RW_K)
    vf, vb = _proj_specs(n_tiles, COL_RW_V)
    of, ob = _out_specs(n_tiles)
    n_tok = proj.shape[0]
    kkf, kkb = _proj_specs(n_tiles, COL_RW_KK)
    consts = (p['rw_k_a'], p['rw_r_k'], p['ones64'],
              p['rw_w0'], p['rw_wa1'], p['rw_w2'], p['rw_a0'], p['rw_a2'], p['rw_g1'], p['rw_g2'])
    out = jax.ShapeDtypeStruct((n_tok, BRANCH_W), F32)
    return pl.pallas_call(
        _rwkv_kernel,
        grid=(n_tiles,),
        in_specs=[rf, kf, vf, kkf, rb, kb, vb, kkb] + [_const_spec(c) for c in consts],
        out_specs=[of, ob, of, of],
        out_shape=[out, out, out, out],
        scratch_shapes=[
            pltpu.VMEM((2, RW_GROUPS, RW_GROUP, RW_GROUP), F32),
            pltpu.VMEM((2, 8, TILE, BRANCH_W), BF16),
            pltpu.VMEM((2, TILE, BRANCH_W), F32),
            pltpu.VMEM((2, TILE, BRANCH_W), F32),
            pltpu.VMEM((RW_ITEMS, CHUNK, RW_GROUP), F32),
            pltpu.VMEM((RW_ITEMS, CHUNK, RW_GROUP), F32),
            pltpu.VMEM((RW_ITEMS, CHUNK, RW_GROUP), BF16),
            pltpu.VMEM((RW_ITEMS, CHUNK, RW_GROUP), BF16),
            pltpu.VMEM((RW_ITEMS, CHUNK, RW_GROUP), BF16),
            pltpu.VMEM((RW_ITEMS, 2 * CHUNK, RW_GROUP), BF16),
            pltpu.VMEM((RW_ITEMS, CHUNK, RW_GROUP), F32),
            pltpu.VMEM((RW_ITEMS, CHUNK, RW_GROUP), F32),
        ],
        compiler_params=_cparams(),
        name="rwkv_scan",
    )(proj, proj, proj, proj, proj, proj, proj, proj, *[_const_array(c) for c in consts])


def _merge_kernel(x_ref, mod_ref, u_ref, gg_ref, lg_ref,
                  s5f_ref, s5b_ref, glf_ref, glb_ref, lrf_ref, lrb_ref, rwf_ref, rwb_ref, bonus_ref, rgate_ref,
                  s5d_ref, glu_ref, gnw_ref, ones64_ref, ones128_ref, lnw_ref, lnb_ref,
                  wg_ref, bg_ref, wb_ref, wo_ref, ln1g_ref, ln1b_ref, o_ref, *, dn_alpha):
    m = _mod_rows(mod_ref, pl.program_id(0) == 0)
    x = x_ref[...]
    h = _bf(x * (1.0 + _mod_part(m, 1)) + _mod_part(m, 0))

    def s5_readout():
        ya = _gelu(s5f_ref[...] + s5b_ref[...] + s5d_ref[...] * u_ref[...])
        return ya * _sigmoid(_mm(ya, glu_ref[...]))

    def gla_readout():
        o = glf_ref[...] + glb_ref[...]
        ms = _seg_sum(o * o, ones128_ref[...]) * (1.0 / GLA_DV)
        return (o * lax.rsqrt(ms + LN_EPS)) * gnw_ref[...] * _silu(gg_ref[...])

    def lru_readout():
        return (lrf_ref[...] + lrb_ref[...]) * _gelu(lg_ref[...])

    def rwkv_readout():
        y = rwf_ref[...] + rwb_ref[...]
        mu = _seg_sum(y, ones64_ref[...]) * (1.0 / RW_N)
        yc0 = y - mu
        var = _seg_sum(yc0 * yc0, ones64_ref[...]) * (1.0 / RW_N)
        yn = (yc0 * lax.rsqrt(var + RW_GN_EPS)) * lnw_ref[...] + lnb_ref[...]
        return (yn + bonus_ref[...]) * rgate_ref[...]

    ys = (s5_readout(), gla_readout(), lru_readout(), rwkv_readout())
    z = jnp.zeros((TILE, D_MODEL), F32)
    for n, y_n in enumerate(ys):
        cols = slice(n * D_MODEL, (n + 1) * D_MODEL)
        gate = _sigmoid(jnp.dot(h, wg_ref[:, cols], preferred_element_type=F32) + bg_ref[:, cols])
        z = z + gate * _mm(y_n, wb_ref[n])
    out = _mm(z, wo_ref[...])
    x1 = dn_alpha * x + _mod_part(m, 2) * out
    o_ref[...] = _layer_norm(x1, LN_EPS) * ln1g_ref[...] + ln1b_ref[...]


def _merge(xs, mod, proj, parts, n_tiles, p, dn_alpha):
    n_tok = xs.shape[0]
    tile_d = pl.BlockSpec((TILE, D_MODEL), lambda i: (i, 0))
    tile_w = pl.BlockSpec((TILE, BRANCH_W), lambda i: (i, 0))

    def pcol(col):
        return pl.BlockSpec((TILE, BRANCH_W), lambda i: (i, col))

    consts = (p['s5_d'], p['s5_w_glu'], p['gla_norm_w'], p['ones64'], p['ones128'], p['rw_ln_w'], p['rw_ln_b'],
              p['w_gate'], p['b_gate'], p['w_branch'], p['w_out'], p['ln1_g'], p['ln1_b'])
    return pl.pallas_call(
        functools.partial(_merge_kernel, dn_alpha=dn_alpha),
        grid=(n_tiles,),
        in_specs=[tile_d, _full(mod), pcol(COL_S5_U), pcol(COL_GLA_G), pcol(COL_LRU_GATE)]
                 + [tile_w] * len(parts) + [_const_spec(c) for c in consts],
        out_specs=tile_d,
        out_shape=jax.ShapeDtypeStruct((n_tok, D_MODEL), F32),
        compiler_params=_cparams(),
        name="merge_ln1",
    )(xs, mod, proj, proj, proj, *parts, *[_const_array(c) for c in consts])


def _ffn_kernel(x_ref, mod_ref, wup_ref, cw_ref, cb_ref, wdn_ref, ln2g_ref, ln2b_ref, o_ref, *,
                dn_alpha, first_tile):
    is_ctx = pl.program_id(0) + first_tile == 0
    m = _mod_rows(mod_ref, is_ctx)
    x = x_ref[...]
    h = _bf(x * (1.0 + _mod_part(m, 4)) + _mod_part(m, 3))
    acc = jnp.zeros((TILE, D_MODEL), F32)
    masks_by_width = {w: _shift_masks((TILE, w), (-1, 1), is_ctx) for w in set(FFN_CHUNKS)}
    start = 0
    for width in FFN_CHUNKS:
        ucols = slice(start, start + width)
        vcols = slice(FFN_HIDDEN + start, FFN_HIDDEN + start + width)
        start += width
        masks = masks_by_width[width]
        u = jnp.dot(h, wup_ref[:, ucols], preferred_element_type=F32)
        v = jnp.dot(h, wup_ref[:, vcols], preferred_element_type=F32)
        uc = (cb_ref[:, ucols] + cw_ref[1:2, ucols] * u
              + cw_ref[0:1, ucols] * _shift_rows(u, -1, masks) + cw_ref[2:3, ucols] * _shift_rows(u, 1, masks))
        acc = acc + _mm(_gelu(uc) * v, wdn_ref[ucols, :])
    x2 = dn_alpha * x + _mod_part(m, 5) * acc
    o_ref[...] = _layer_norm(x2, LN_EPS) * ln2g_ref[...] + ln2b_ref[...]


def _ffn(xs, mod, p, dn_alpha, skip_ctx):
    first = 1 if skip_ctx else 0
    n_out = xs.shape[0] - first * TILE
    consts = (p['ffn_w_up'], p['ffn_conv_w'], p['ffn_conv_b'], p['ffn_w_down'], p['ln2_g'], p['ln2_b'])
    return pl.pallas_call(
        functools.partial(_ffn_kernel, dn_alpha=dn_alpha, first_tile=first),
        grid=(n_out // TILE,),
        in_specs=[pl.BlockSpec((TILE, D_MODEL), lambda i: (i + first, 0)), _full(mod)] + [_const_spec(c) for c in consts],
        out_specs=pl.BlockSpec((TILE, D_MODEL), lambda i: (i, 0)),
        out_shape=jax.ShapeDtypeStruct((n_out, D_MODEL), F32),
        compiler_params=_cparams(),
        name="convffn_ln2",
    )(xs, mod, *[_const_array(c) for c in consts])


def _block_diag(blocks):
    n, a, b = blocks.shape
    eye = jnp.eye(n, dtype=blocks.dtype)
    return (eye[:, None, :, None] * blocks[:, :, None, :]).reshape(n * a, n * b)


def _pad_cols(w, n):
    return jnp.pad(w, [(0, 0)] * (w.ndim - 1) + [(0, n - w.shape[-1])])


def _pad_rows(w, n):
    return jnp.pad(w, [(0, 0)] * (w.ndim - 2) + [(0, n - w.shape[-2]), (0, 0)])


def _s5_perms():
    perm = np.zeros((2, TILE, TILE), np.float32)
    for j in range(S5_STEPS):
        for s in range(S5_SUB):
            t = s * S5_STEPS + j
            perm[0, j * S5_SUB + s, t] = 1.0
            perm[1, j * S5_SUB + s, TILE - 1 - t] = 1.0
    return jnp.asarray(perm, BF16), jnp.asarray(np.transpose(perm, (0, 2, 1)), BF16)


def _ones_blocks(width):
    idx = np.arange(BRANCH_W) // width
    return jnp.asarray(idx[:, None] == idx[None, :], BF16)


STACKED_WEIGHTS = ('w_in', 'w_gate', 'w_branch', 'w_out', 'ffn_w_up', 'ffn_w_down', 's5_w_glu', 'rw_g1', 'rw_g2')


def _layer_params(q, stacked, i):
    p = {name: _LayerOf(stacked[name], i) for name in STACKED_WEIGHTS}
    row = lambda v: v[i].reshape(1, -1)
    dt = jnp.exp(q['s5_log_step'][i])[:, :, None]
    lam_re, lam_im = q['s5_lam_re'][i], q['s5_lam_im'][i]
    mag = jnp.exp(lam_re * dt)
    a_re, a_im = mag * jnp.cos(lam_im * dt), mag * jnp.sin(lam_im * dt)
    den = lam_re * lam_re + lam_im * lam_im
    f_re = ((a_re - 1.0) * lam_re + a_im * lam_im) / den
    f_im = (a_im * lam_re - (a_re - 1.0) * lam_im) / den
    b_re, b_im = q['s5_b_re'][i], q['s5_b_im'][i]
    d_re = f_re[..., None] * b_re[None] - f_im[..., None] * b_im[None]
    d_im = f_re[..., None] * b_im[None] + f_im[..., None] * b_re[None]
    gh = S5_G // 2
    halves = [slice(0, gh), slice(gh, S5_G)]
    bd = lambda blocks: _block_diag(jnp.swapaxes(blocks, 1, 2))
    p['s5_drive'] = jnp.stack([jnp.stack([jnp.concatenate([bd(d_re[dr, hs]), bd(d_im[dr, hs])], axis=1)
                                          for hs in halves]) for dr in range(2)]).astype(BF16)
    p['s5_a'] = jnp.concatenate([jnp.concatenate([a_re[:, hs].reshape(2, 1, -1), a_im[:, hs].reshape(2, 1, -1)],
                                                 axis=-1) for hs in halves], axis=-1)
    c_re, c_im = q['s5_c_re'][i], q['s5_c_im'][i]
    p['s5_read'] = jnp.stack([jnp.concatenate([bd(c_re[hs]), -bd(c_im[hs])], axis=0)
                              for hs in halves]).astype(BF16)
    p['s5_perm'], p['s5_perm_t'] = _s5_perms()
    p['s5_d'] = row(q['s5_d'])
    p['gla_w1'] = _pad_cols(q['gla_w1'][i], LORA_PAD).astype(BF16)
    p['gla_w2'] = _pad_rows(q['gla_w2'][i], LORA_PAD).astype(BF16)
    p['gla_bias'] = q['gla_bias'][i]
    p['gla_norm_w'] = jnp.tile(q['gla_norm_w'][i], GLA_HEADS).reshape(1, -1)
    p['lru_conv_w'] = q['lru_conv_w'][i]
    p['lru_conv_b'] = row(q['lru_conv_b'])
    p['lru_wa'] = jnp.stack([_block_diag(q['lru_w_a'][i, dr]) for dr in range(2)]).astype(BF16)
    p['lru_wx'] = jnp.stack([_block_diag(q['lru_w_x'][i, dr]) for dr in range(2)]).astype(BF16)
    p['lru_b_a'], p['lru_b_x'] = q['lru_b_a'][i], q['lru_b_x'][i]
    p['lru_c'] = -LRU_C * jax.nn.softplus(-q['lru_lam'][i])
    p['rw_mu'] = q['rw_mu'][i]
    for name in ('rw_k_k', 'rw_k_a', 'rw_r_k', 'rw_ln_w', 'rw_ln_b'):
        p[name] = row(q[name])
    p['rw_w0'], p['rw_a0'] = q['rw_w0'][i], q['rw_a0'][i]
    p['rw_wa1'] = jnp.concatenate([_pad_cols(q['rw_w1'][i], LORA_PAD), _pad_cols(q['rw_a1'][i], LORA_PAD)],
                                  axis=-1).astype(BF16)
    p['rw_w2'] = _pad_rows(q['rw_w2'][i], LORA_PAD).astype(BF16)
    p['rw_a2'] = _pad_rows(q['rw_a2'][i], LORA_PAD).astype(BF16)
    p['ones64'] = _ones_blocks(RW_N)
    p['ones128'] = _ones_blocks(GLA_DV)
    p['b_gate'] = row(q['b_gate'])
    p['ffn_conv_w'] = q['ffn_conv_w'][i]
    p['ffn_conv_b'] = row(q['ffn_conv_b'])
    for name in ('ln1_g', 'ln1_b', 'ln2_g', 'ln2_b'):
        p[name] = row(q[name])
    return p


def kernel(x, c, ctx, c_ctx, w_mod, b_mod, w_in, s5_lam_re, s5_lam_im, s5_log_step, s5_b_re, s5_b_im, s5_c_re, s5_c_im, s5_d, s5_w_glu, gla_w1, gla_w2, gla_bias, gla_norm_w, lru_conv_w, lru_conv_b, lru_w_a, lru_b_a, lru_w_x, lru_b_x, lru_lam, rw_mu, rw_w0, rw_w1, rw_w2, rw_a0, rw_a1, rw_a2, rw_g1, rw_g2, rw_k_k, rw_k_a, rw_r_k, rw_ln_w, rw_ln_b, w_branch, w_gate, b_gate, w_out, ln1_g, ln1_b, ln2_g, ln2_b, ffn_w_up, ffn_conv_w, ffn_conv_b, ffn_w_down):
    q = dict(w_in=w_in, s5_lam_re=s5_lam_re, s5_lam_im=s5_lam_im, s5_log_step=s5_log_step, s5_b_re=s5_b_re,
             s5_b_im=s5_b_im, s5_c_re=s5_c_re, s5_c_im=s5_c_im, s5_d=s5_d, s5_w_glu=s5_w_glu,
             gla_w1=gla_w1, gla_w2=gla_w2, gla_bias=gla_bias, gla_norm_w=gla_norm_w,
             lru_conv_w=lru_conv_w, lru_conv_b=lru_conv_b, lru_w_a=lru_w_a, lru_b_a=lru_b_a,
             lru_w_x=lru_w_x, lru_b_x=lru_b_x, lru_lam=lru_lam,
             rw_mu=rw_mu, rw_w0=rw_w0, rw_w1=rw_w1, rw_w2=rw_w2, rw_a0=rw_a0, rw_a1=rw_a1, rw_a2=rw_a2,
             rw_g1=rw_g1, rw_g2=rw_g2, rw_k_k=rw_k_k, rw_k_a=rw_k_a, rw_r_k=rw_r_k,
             rw_ln_w=rw_ln_w, rw_ln_b=rw_ln_b,
             w_branch=w_branch, w_gate=w_gate, b_gate=b_gate, w_out=w_out,
             ln1_g=ln1_g, ln1_b=ln1_b, ln2_g=ln2_g, ln2_b=ln2_b,
             ffn_w_up=ffn_w_up, ffn_conv_w=ffn_conv_w, ffn_conv_b=ffn_conv_b, ffn_w_down=ffn_w_down)
    batch, seq, d = x.shape
    assert batch == 1 and d == D_MODEL and ctx.shape[1] == CTX_LEN and seq % TILE == 0
    depth = w_mod.shape[0]
    dn_alpha = (2 * depth) ** 0.25
    n_tiles = 1 + seq // TILE

    cc = jnp.zeros((8, D_MODEL), F32).at[0].set(c[0]).at[1].set(c_ctx)
    mods = _modulation(cc, w_mod, b_mod)
    xs = _ln0(ctx[0], x[0])
    stacked = {name: q[name].astype(BF16) for name in STACKED_WEIGHTS}
    for i in range(depth):
        p = _layer_params(q, stacked, i)
        mod = mods[i]
        proj = _inproj(xs, mod, p)
        s5f, s5b = _s5(proj, n_tiles, p)
        glf, glb = _gla(proj, n_tiles, p)
        lrf, lrb = _lru(proj, n_tiles, p)
        rwf, rwb, bonus, rgate = _rwkv(proj, n_tiles, p)
        xs = _merge(xs, mod, proj, (s5f, s5b, glf, glb, lrf, lrb, rwf, rwb, bonus, rgate), n_tiles, p, dn_alpha)
        xs = _ffn(xs, mod, p, dn_alpha, skip_ctx=i == depth - 1)
    return xs.reshape(batch, seq, d)
```
